```python
import jax, jax.numpy as jnp
from jax import lax
import numpy as np

D_MODEL = 2048
BATCH = 4
SEQ = 4096
DEPTH = 2

GRID_W = 64
CTX_LEN = 256
W_RG = 2048
RG_BLOCKS = 16
RG_BLOCK = W_RG // RG_BLOCKS
RG_C = 8.0
CONV_W = 4
M_HEADS = 8
M_HEAD_DIM = 256
W_M = M_HEADS * M_HEAD_DIM
M_CHUNK = 128
C_RGG = W_RG
C_M = 2 * W_RG
M_PART = 4 * W_M + 4 * M_HEADS
C_G = C_M + M_PART
N_IN = C_G + 2 * D_MODEL
D_FF = 7168
N_EXPERTS = 8
TOP_K = 2
MOE_BLOCK = 256
N_DENSE = (DEPTH + 1) // 2
N_MOE = DEPTH // 2
ALPHA = (2.0 * DEPTH) ** 0.25
BETA = (8.0 * DEPTH) ** -0.25
LN_EPS = 1e-6

kernel_name = "hybrid_rglru_mlstm_moe_diffusion_block"


def layer_norm(t):
    tf = t.astype(jnp.float32)
    mu = jnp.mean(tf, axis=-1, keepdims=True)
    var = jnp.mean(jnp.square(tf - mu), axis=-1, keepdims=True)
    return ((tf - mu) * lax.rsqrt(var + LN_EPS)).astype(t.dtype)


def ln_affine(t, g, b):
    return layer_norm(t) * g + b


def modulate(t, shift, scale):
    return layer_norm(t) * (1.0 + scale) + shift


def centred_dwconv(t, w, b):
    L = t.shape[1]
    tp = jnp.pad(t, ((0, 0), (CONV_W // 2, CONV_W - 1 - CONV_W // 2), (0, 0)))
    y = b
    for j in range(CONV_W):
        y = y + w[j] * tp[:, j:j + L]
    return y


def to_colmajor(t):
    B, L, C = t.shape
    rows = L // GRID_W
    return t.reshape(B, rows, GRID_W, C).transpose(0, 2, 1, 3).reshape(B, L, C)


def from_colmajor(t):
    B, L, C = t.shape
    rows = L // GRID_W
    return t.reshape(B, GRID_W, rows, C).transpose(0, 2, 1, 3).reshape(B, L, C)


def _lin_combine(e1, e2):
    a1, b1 = e1
    a2, b2 = e2
    return a1 * a2, a2 * b1 + b2


def linear_scan(a, b, h0, reverse):
    if reverse:
        b = b.at[:, -1].add(a[:, -1] * h0)
    else:
        b = b.at[:, 0].add(a[:, 0] * h0)
    _, h = lax.associative_scan(_lin_combine, (a, b), axis=1, reverse=reverse)
    return h


def rglru_coeffs(xc, wa, ba, wx, bx, lam):
    B, L, _ = xc.shape
    xh = xc.reshape(B, L, RG_BLOCKS, RG_BLOCK)
    r = jax.nn.sigmoid(jnp.einsum('blhi,hij->blhj', xh, wa).reshape(B, L, W_RG) + ba)
    i = jax.nn.sigmoid(jnp.einsum('blhi,hij->blhj', xh, wx).reshape(B, L, W_RG) + bx)
    log_a = -RG_C * r * jax.nn.softplus(-lam)
    a = jnp.exp(log_a)
    b = jnp.sqrt(-jnp.expm1(2.0 * log_a)) * (i * xc)
    return a, b


def rglru_bidir(xc, gate_pre, h0, wa, ba, wx, bx, lam):
    xc = xc.astype(jnp.float32)
    hs = []
    for d, rev in ((0, False), (1, True)):
        a, b = rglru_coeffs(xc, wa[d], ba[d], wx[d], bx[d], lam[d])
        hs.append(linear_scan(a, b, h0[d], rev))
    y = (hs[0] + hs[1]) * jax.nn.gelu(gate_pre.astype(jnp.float32))
    return y, (hs[0][:, -1], hs[1][:, 0])


def mlstm_chunkwise(q, k, v, ig, lf, state):
    B, H, L, DH = q.shape
    nc = L // M_CHUNK

    def chunks(t):
        return jnp.moveaxis(t.reshape((B, H, nc, M_CHUNK) + t.shape[3:]), 2, 0)

    mask = jnp.tril(jnp.ones((M_CHUNK, M_CHUNK), dtype=bool))

    def step(carry, inp):
        C, n, m = carry
        qc, kc, vc, ic, fc = inp
        bcum = jnp.cumsum(fc, axis=-1)
        log_d = jnp.where(mask, bcum[..., :, None] - bcum[..., None, :] + ic[..., None, :], -jnp.inf)
        log_inter = bcum + m[..., None]
        m_t = jnp.maximum(jnp.max(log_d, axis=-1), log_inter)
        s = jnp.einsum('bhtd,bhsd->bhts', qc, kc) * jnp.exp(log_d - m_t[..., None])
        w_inter = jnp.exp(log_inter - m_t)
        num = jnp.einsum('bhts,bhsd->bhtd', s, vc) + w_inter[..., None] * jnp.einsum('bhed,bhtd->bhte', C, qc)
        den = jnp.sum(s, axis=-1) + w_inter * jnp.einsum('bhd,bhtd->bht', n, qc)
        h = num / jnp.maximum(jnp.abs(den), jnp.exp(-m_t))[..., None]
        b_last = bcum[..., -1]
        log_w = b_last[..., None] - bcum + ic
        m_new = jnp.maximum(b_last + m, jnp.max(log_w, axis=-1))
        w = jnp.exp(log_w - m_new[..., None])
        decay = jnp.exp(b_last + m - m_new)
        C = decay[..., None, None] * C + jnp.einsum('bhs,bhse,bhsd->bhed', w, vc, kc)
        n = decay[..., None] * n + jnp.einsum('bhs,bhsd->bhd', w, kc)
        return (C, n, m_new), h

    state, h = lax.scan(step, state, (chunks(q), chunks(k), chunks(v), chunks(ig), chunks(lf)))
    h = jnp.moveaxis(h, 0, 2).reshape(B, H, L, DH)
    return h, state


def mlstm_zero_state(B):
    z = (jnp.zeros((B, M_HEADS, M_HEAD_DIM, M_HEAD_DIM), jnp.float32),
         jnp.zeros((B, M_HEADS, M_HEAD_DIM), jnp.float32),
         jnp.zeros((B, M_HEADS), jnp.float32))
    return (z, z)


def mlstm_branch(mp, state0, conv_w, conv_b, gate_b, gn_g):
    B, L, _ = mp.shape
    qk = jax.nn.silu(centred_dwconv(mp[..., :2 * W_M], conv_w, conv_b))
    v = mp[..., 2 * W_M:3 * W_M]
    o = mp[..., 3 * W_M:4 * W_M]
    g = mp[..., 4 * W_M:].reshape(B, L, 2, 2, M_HEADS).astype(jnp.float32) + gate_b

    def heads(t):
        return t.reshape(B, L, M_HEADS, M_HEAD_DIM).transpose(0, 2, 1, 3).astype(jnp.float32)

    q = heads(qk[..., :W_M]) * (M_HEAD_DIM ** -0.5)
    k = heads(qk[..., W_M:])
    v = heads(v)
    ig = g[:, :, :, 0].transpose(0, 2, 3, 1)
    lf = jax.nn.log_sigmoid(g[:, :, :, 1]).transpose(0, 2, 3, 1)
    flip = lambda t: jnp.flip(t, axis=2)
    h_f, st_f = mlstm_chunkwise(q, k, v, ig[:, 0], lf[:, 0], state0[0])
    h_b, st_b = mlstm_chunkwise(flip(q), flip(k), flip(v), flip(ig[:, 1]), flip(lf[:, 1]), state0[1])
    h = h_f + flip(h_b)
    mu = jnp.mean(h, axis=-1, keepdims=True)
    var = jnp.mean(jnp.square(h - mu), axis=-1, keepdims=True)
    h = ((h - mu) * lax.rsqrt(var + LN_EPS)).transpose(0, 2, 1, 3).reshape(B, L, W_M) * gn_g
    y = h * jax.nn.sigmoid(o.astype(jnp.float32))
    return y, (st_f, st_b)


def mixer(u_c, u_l, w_in, conv_rg_w, conv_rg_b, conv_m_w, conv_m_b, rg_wa, rg_ba, rg_wx, rg_bx, rg_lam,
          m_gate_b, m_gn_g, p_rg, p_m, w_out, ctx_out):
    B = u_l.shape[0]
    pc = u_c @ w_in
    pl = u_l @ w_in
    z = jnp.zeros((B, W_RG), jnp.float32)
    rgp = (rg_wa, rg_ba, rg_wx, rg_bx, rg_lam)
    yr_c, st_rg = rglru_bidir(centred_dwconv(pc[..., :W_RG], conv_rg_w, conv_rg_b), pc[..., C_RGG:C_M], (z, z), *rgp)
    yr_l, _ = rglru_bidir(centred_dwconv(pl[..., :W_RG], conv_rg_w, conv_rg_b), pl[..., C_RGG:C_M], st_rg, *rgp)
    ym_c, st_m = mlstm_branch(pc[..., C_M:C_G], mlstm_zero_state(B), conv_m_w, conv_m_b, m_gate_b, m_gn_g)
    ym_l, _ = mlstm_branch(to_colmajor(pl[..., C_M:C_G]), st_m, conv_m_w, conv_m_b, m_gate_b, m_gn_g)
    ym_l = from_colmajor(ym_l)

    def merge(p, yr, ym):
        g = jax.nn.sigmoid(p[..., C_G:])
        mix = g[..., :D_MODEL] * (yr.astype(p.dtype) @ p_rg) + g[..., D_MODEL:] * (ym.astype(p.dtype) @ p_m)
        return mix @ w_out

    out_l = merge(pl, yr_l, ym_l)
    out_c = merge(pc, yr_c, ym_c) if ctx_out else None
    return out_c, out_l


def swiglu(t, w1, w3, w2):
    return (jax.nn.silu(t @ w1) * (t @ w3)) @ w2


def moe_swiglu(t, router_w, router_b, w1, w3, w2):
    n = t.shape[0]
    logits = (t @ router_w).astype(jnp.float32) + router_b
    top_val, top_idx = lax.top_k(logits, TOP_K)
    gates = jax.nn.softmax(top_val, axis=-1)
    expert = top_idx.reshape(-1)
    token = jnp.repeat(jnp.arange(n), TOP_K)
    weight = gates.reshape(-1)
    n_assign = n * TOP_K
    order = jnp.argsort(expert)
    e_s, tok_s, w_s = expert[order], token[order], weight[order]
    counts = jnp.bincount(expert, length=N_EXPERTS)
    padded = (counts + MOE_BLOCK - 1) // MOE_BLOCK * MOE_BLOCK
    pend = jnp.cumsum(padded)
    pstart = pend - padded
    sstart = jnp.cumsum(counts) - counts
    pos = pstart[e_s] + jnp.arange(n_assign) - sstart[e_s]
    n_blocks = -(-n_assign // MOE_BLOCK) + N_EXPERTS
    xb = jnp.zeros((n_blocks * MOE_BLOCK, t.shape[1]), t.dtype).at[pos].set(t[tok_s])
    block_e = jnp.minimum(jnp.searchsorted(pend, jnp.arange(n_blocks) * MOE_BLOCK, side='right'), N_EXPERTS - 1)

    def run(args):
        xg, e = args
        return swiglu(xg, w1[e], w3[e], w2[e])

    yb = lax.map(run, (xb.reshape(n_blocks, MOE_BLOCK, -1), block_e)).reshape(n_blocks * MOE_BLOCK, -1)
    return jnp.zeros_like(t).at[tok_s].add((yb[pos] * w_s[:, None]).astype(t.dtype))


def setup_inputs(seed: int = 0) -> dict:
    key = jax.random.key(seed)
    ks = jax.random.split(key, 32)
    f32 = jnp.float32

    def nrm(k, shape, scale):
        return jax.random.normal(k, shape, f32) * scale

    u = jax.random.uniform(ks[15], (DEPTH, 2, W_RG), f32, 0.9, 0.999)
    a0 = u ** (1.0 / RG_C)
    rg_lam = jnp.log(a0) - jnp.log1p(-a0)
    ib = nrm(ks[16], (DEPTH, 2, M_HEADS), 0.1)
    fb = jnp.linspace(3.0, 6.0, M_HEADS, dtype=f32) + nrm(ks[17], (DEPTH, 2, M_HEADS), 0.1)
    return {
        "x": nrm(ks[0], (BATCH, SEQ, D_MODEL), 1.0),
        "c": nrm(ks[1], (BATCH, D_MODEL), 1.0),
        "ctx": nrm(ks[2], (BATCH, CTX_LEN, D_MODEL), 1.0),
        "c_ctx": nrm(ks[3], (D_MODEL,), 1.0),
        "w_mod": nrm(ks[4], (DEPTH, D_MODEL, 6 * D_MODEL), 0.5 * D_MODEL ** -0.5),
        "b_mod": nrm(ks[5], (DEPTH, 6 * D_MODEL), 0.01),
        "w_in": nrm(ks[6], (DEPTH, D_MODEL, N_IN), D_MODEL ** -0.5),
        "conv_rg_w": nrm(ks[7], (DEPTH, CONV_W, W_RG), CONV_W ** -0.5),
        "conv_rg_b": nrm(ks[8], (DEPTH, W_RG), 0.01),
        "conv_m_w": nrm(ks[9], (DEPTH, CONV_W, 2 * W_M), CONV_W ** -0.5),
        "conv_m_b": nrm(ks[10], (DEPTH, 2 * W_M), 0.01),
        "rg_wa": nrm(ks[11], (DEPTH, 2, RG_BLOCKS, RG_BLOCK, RG_BLOCK), RG_BLOCK ** -0.5),
        "rg_ba": nrm(ks[12], (DEPTH, 2, W_RG), 0.01),
        "rg_wx": nrm(ks[13], (DEPTH, 2, RG_BLOCKS, RG_BLOCK, RG_BLOCK), RG_BLOCK ** -0.5),
        "rg_bx": nrm(ks[14], (DEPTH, 2, W_RG), 0.01),
        "rg_lam": rg_lam,
        "m_gate_b": jnp.stack([ib, fb], axis=2),
        "m_gn_g": 1.0 + nrm(ks[18], (DEPTH, W_M), 0.01),
        "p_rg": nrm(ks[19], (DEPTH, W_RG, D_MODEL), W_RG ** -0.5),
        "p_m": nrm(ks[20], (DEPTH, W_M, D_MODEL), W_M ** -0.5),
        "w_out": nrm(ks[21], (DEPTH, D_MODEL, D_MODEL), BETA * D_MODEL ** -0.5),
        "ln_g": 1.0 + nrm(ks[22], (DEPTH, 2, D_MODEL), 0.01),
        "ln_b": nrm(ks[23], (DEPTH, 2, D_MODEL), 0.01),
        "ff_w1": nrm(ks[24], (N_DENSE, D_MODEL, D_FF), D_MODEL ** -0.5),
        "ff_w3": nrm(ks[25], (N_DENSE, D_MODEL, D_FF), D_MODEL ** -0.5),
        "ff_w2": nrm(ks[26], (N_DENSE, D_FF, D_MODEL), BETA * D_FF ** -0.5),
        "router_w": nrm(ks[27], (N_MOE, D_MODEL, N_EXPERTS), D_MODEL ** -0.5),
        "router_b": nrm(ks[28], (N_MOE, N_EXPERTS), 0.01),
        "ex_w1": nrm(ks[29], (N_MOE, N_EXPERTS, D_MODEL, D_FF), D_MODEL ** -0.5),
        "ex_w3": nrm(ks[30], (N_MOE, N_EXPERTS, D_MODEL, D_FF), D_MODEL ** -0.5),
        "ex_w2": nrm(ks[31], (N_MOE, N_EXPERTS, D_FF, D_MODEL), BETA * D_FF ** -0.5),
    }


def reference(x, c, ctx, c_ctx, w_mod, b_mod, w_in, conv_rg_w, conv_rg_b, conv_m_w, conv_m_b, rg_wa, rg_ba,
              rg_wx, rg_bx, rg_lam, m_gate_b, m_gn_g, p_rg, p_m, w_out, ln_g, ln_b, ff_w1, ff_w3, ff_w2,
              router_w, router_b, ex_w1, ex_w3, ex_w2):
    B, L, D = x.shape
    h_l, h_c = x, ctx
    s_lat = jax.nn.silu(c)
    s_ctx = jax.nn.silu(c_ctx)
    for l in range(DEPTH):
        last = l == DEPTH - 1
        mod_l = (s_lat @ w_mod[l] + b_mod[l])[:, None, :]
        mod_c = s_ctx @ w_mod[l] + b_mod[l]
        sh1_l, sc1_l, g1_l, sh2_l, sc2_l, g2_l = jnp.split(mod_l, 6, axis=-1)
        sh1_c, sc1_c, g1_c, sh2_c, sc2_c, g2_c = jnp.split(mod_c, 6, axis=-1)
        o_c, o_l = mixer(modulate(h_c, sh1_c, sc1_c), modulate(h_l, sh1_l, sc1_l), w_in[l],
                         conv_rg_w[l], conv_rg_b[l], conv_m_w[l], conv_m_b[l], rg_wa[l], rg_ba[l], rg_wx[l],
                         rg_bx[l], rg_lam[l], m_gate_b[l], m_gn_g[l], p_rg[l], p_m[l], w_out[l], not last)
        h_l = ln_affine(ALPHA * h_l + g1_l * o_l, ln_g[l, 0], ln_b[l, 0])
        v_l = modulate(h_l, sh2_l, sc2_l).reshape(B * L, D)
        if last:
            t = v_l
        else:
            h_c = ln_affine(ALPHA * h_c + g1_c * o_c, ln_g[l, 0], ln_b[l, 0])
            v_c = modulate(h_c, sh2_c, sc2_c).reshape(B * CTX_LEN, D)
            t = jnp.concatenate([v_c, v_l], axis=0)
        if l % 2 == 0:
            f = swiglu(t, ff_w1[l // 2], ff_w3[l // 2], ff_w2[l // 2])
        else:
            f = moe_swiglu(t, router_w[l // 2], router_b[l // 2], ex_w1[l // 2], ex_w3[l // 2], ex_w2[l // 2])
        n_c = 0 if last else B * CTX_LEN
        f_l = f[n_c:].reshape(B, L, D)
        h_l = ln_affine(ALPHA * h_l + g2_l * f_l, ln_g[l, 1], ln_b[l, 1])
        if not last:
            f_c = f[:n_c].reshape(B, CTX_LEN, D)
            h_c = ln_affine(ALPHA * h_c + g2_c * f_c, ln_g[l, 1], ln_b[l, 1])
    return h_l
```

```python
import functools

import jax
import jax.numpy as jnp
from jax import lax
from jax.experimental import pallas as pl
from jax.experimental.pallas import tpu as pltpu

F32 = jnp.float32
BF16 = jnp.bfloat16

D_MODEL = 2048
BATCH = 4
SEQ = 4096
DEPTH = 2
GRID_W = 64
CTX_LEN = 256
W_RG = 2048
RG_BLOCKS = 16
RG_BLOCK = W_RG // RG_BLOCKS
RG_C = 8.0
CONV_W = 4
M_HEADS = 8
M_HEAD_DIM = 256
W_M = M_HEADS * M_HEAD_DIM
M_CHUNK = 128
C_M = 2 * W_RG
N_GATES = 4 * M_HEADS
D_FF = 7168
N_EXPERTS = 8
TOP_K = 2
ALPHA = (2.0 * DEPTH) ** 0.25
LN_EPS = 1e-6

N_LAT = BATCH * SEQ
N_CTX = BATCH * CTX_LEN
N_TOK = N_LAT + N_CTX
LANES = 128
SUBLANES = 8
VMEM_LIMIT = 52 * 1024 * 1024

COL_XR, COL_RGG, COL_Q, COL_K, COL_V, COL_O, COL_GA, COL_GB = range(8)
N_MAIN = 8 * D_MODEL

SEQ_ALL = CTX_LEN + SEQ
CONV_PAD = 8
SEQ_PADDED = CONV_PAD + CTX_LEN + CONV_PAD + SEQ + CONV_PAD
MOE_TILE = 512
MOE_TILES = (N_LAT * TOP_K) // MOE_TILE + N_EXPERTS


def _cparams(*sem):
    return pltpu.CompilerParams(dimension_semantics=sem, vmem_limit_bytes=VMEM_LIMIT)


def _layer_norm(x):
    mu = jnp.mean(x, axis=-1, keepdims=True)
    xc = x - mu
    var = jnp.mean(xc * xc, axis=-1, keepdims=True)
    return xc * lax.rsqrt(var + LN_EPS)


def _sigmoid(x):
    return 1.0 / (1.0 + jnp.exp(-x))


def _softplus(x):
    return jnp.maximum(x, 0.0) + jnp.log1p(jnp.exp(-jnp.abs(x)))


def _mod_row(i, tm):
    return jnp.where(i >= N_LAT // tm, BATCH, i // (SEQ // tm))


def _mod_spec(chunk, tm):
    return pl.BlockSpec((1, 1, D_MODEL), lambda i, *_: (_mod_row(i, tm), 0, chunk))


def _modvec_kernel(s_ref, w_ref, b_ref, o_ref):
    s = s_ref[...]
    s = s * _sigmoid(s)
    o_ref[...] = jnp.dot(s.astype(BF16), w_ref[...].astype(BF16), preferred_element_type=F32) + b_ref[...]


def _modvec(cond, w, b):
    n = w.shape[1]
    tn = 1024
    out = pl.pallas_call(
        _modvec_kernel,
        grid=(n // tn,),
        in_specs=[pl.BlockSpec((SUBLANES, D_MODEL), lambda j: (0, 0)),
                  pl.BlockSpec((D_MODEL, tn), lambda j: (0, j)),
                  pl.BlockSpec((1, tn), lambda j: (0, j))],
        out_specs=pl.BlockSpec((SUBLANES, tn), lambda j: (0, j)),
        out_shape=jax.ShapeDtypeStruct((SUBLANES, n), F32),
        compiler_params=_cparams("arbitrary"),
        name="modvec",
    )(cond, w, b.reshape(1, n))
    return out.reshape(SUBLANES, 1, n)


def _proj_kernel(x_ref, sh_ref, sc_ref, w_ref, o_ref, u_ref):
    @pl.when(pl.program_id(1) == 0)
    def _():
        u = _layer_norm(x_ref[...]) * (1.0 + sc_ref[0]) + sh_ref[0]
        u_ref[...] = u.astype(BF16)

    o_ref[...] = jnp.dot(u_ref[...], w_ref[...], preferred_element_type=F32)


def _proj(h, mod, w, tm, tn):
    r, n = h.shape[0], w.shape[1]
    return pl.pallas_call(
        _proj_kernel,
        grid=(r // tm, n // tn),
        in_specs=[pl.BlockSpec((tm, D_MODEL), lambda i, j: (i, 0)),
                  _mod_spec(0, tm), _mod_spec(1, tm),
                  pl.BlockSpec((D_MODEL, tn), lambda i, j: (0, j))],
        out_specs=pl.BlockSpec((tm, tn), lambda i, j: (i, j)),
        out_shape=jax.ShapeDtypeStruct((r, n), F32),
        scratch_shapes=[pltpu.VMEM((tm, D_MODEL), BF16)],
        compiler_params=_cparams("arbitrary", "arbitrary"),
        name="proj_in",
    )(h, mod, mod, w)


def _rg_kernel(x_ref, g_ref, cw_ref, cb_ref, w_ref, bias_ref, lam_ref, h0_ref, y_ref, hl_ref,
               xs, a0, b0, a1, b1, *, seq, tc, seg, pitch):
    zeros8 = jnp.zeros((CONV_PAD, LANES), F32)
    xs[0:CONV_PAD, :] = zeros8
    xs[CONV_PAD + seq:CONV_PAD + seq + CONV_PAD, :] = zeros8
    xs[CONV_PAD:CONV_PAD + seq, :] = x_ref[...]

    cw = cw_ref[...]
    sp = _softplus(-lam_ref[...])
    dec = ((a0, b0), (a1, b1))
    for c in range(seq // tc):
        r0 = c * tc
        xc = cb_ref[...] + cw[0:1, :] * xs[r0 + CONV_PAD - 2:r0 + CONV_PAD - 2 + tc, :]
        for j in range(1, CONV_W):
            xc = xc + cw[j:j + 1, :] * xs[r0 + CONV_PAD - 2 + j:r0 + CONV_PAD - 2 + j + tc, :]
        z = jnp.dot(xc.astype(BF16), w_ref[...], preferred_element_type=F32) + bias_ref[...]
        for d in range(2):
            rg = _sigmoid(z[:, (2 * d) * LANES:(2 * d + 1) * LANES])
            ig = _sigmoid(z[:, (2 * d + 1) * LANES:(2 * d + 2) * LANES])
            log_a = (-RG_C) * rg * sp[d:d + 1, :]
            a = jnp.exp(log_a)
            b = jnp.sqrt(jnp.tanh(-log_a) * (1.0 + a * a)) * (ig * xc)
            a_ref, b_ref = dec[d]
            for p in range(max(tc // seg, 1)):
                n = min(tc, seg)
                t = r0 + p * n
                dst = (t // seg) * pitch + t % seg
                a_ref[dst:dst + n, :] = a[p * n:(p + 1) * n, :]
                b_ref[dst:dst + n, :] = b[p * n:(p + 1) * n, :]

    def scan_step(i, carry):
        hf, pf, hb, pb = carry
        tf = i
        tb = seg - 1 - i
        af = a0[pl.ds(tf, SUBLANES, stride=pitch), :]
        bf = b0[pl.ds(tf, SUBLANES, stride=pitch), :]
        ab = a1[pl.ds(tb, SUBLANES, stride=pitch), :]
        bb = b1[pl.ds(tb, SUBLANES, stride=pitch), :]
        hf = af * hf + bf
        pf = pf * af
        hb = ab * hb + bb
        pb = pb * ab
        b0[pl.ds(tf, SUBLANES, stride=pitch), :] = hf
        a0[pl.ds(tf, SUBLANES, stride=pitch), :] = pf
        b1[pl.ds(tb, SUBLANES, stride=pitch), :] = hb
        a1[pl.ds(tb, SUBLANES, stride=pitch), :] = pb
        return hf, pf, hb, pb

    z8 = jnp.zeros((SUBLANES, LANES), F32)
    o8 = jnp.ones((SUBLANES, LANES), F32)
    hf, pf, hb, pb = lax.fori_loop(0, seg, scan_step, (z8, o8, z8, o8))

    h0 = h0_ref[...]
    cf = [None] * SUBLANES
    cbk = [None] * SUBLANES
    carry = h0[0:1, :]
    for s in range(SUBLANES):
        cf[s] = carry
        carry = pf[s:s + 1, :] * carry + hf[s:s + 1, :]
    hl_f = carry
    carry = h0[1:2, :]
    for s in reversed(range(SUBLANES)):
        cbk[s] = carry
        carry = pb[s:s + 1, :] * carry + hb[s:s + 1, :]
    hl_b = carry
    hl_ref[...] = jnp.concatenate([hl_f, hl_b], axis=0)

    for s in range(SUBLANES):
        for c in range(max(seg // tc, 1)):
            n = min(tc, seg)
            src = s * pitch + c * n
            t = s * seg + c * n
            h = (b0[src:src + n, :] + a0[src:src + n, :] * cf[s]
                 + b1[src:src + n, :] + a1[src:src + n, :] * cbk[s])
            y_ref[t:t + n, :] = (h * jax.nn.gelu(g_ref[t:t + n, :])).astype(y_ref.dtype)


def _rg_branch(p, y_prev, h0, cw, cb, wcat, bcat, lam, *, seq, row_block0):
    tc = min(256, seq)
    seg = seq // SUBLANES
    pitch = seg + SUBLANES if (seg // SUBLANES) % 2 == 0 else seg
    nblk = W_RG // LANES
    rows = SUBLANES * pitch
    kern = functools.partial(_rg_kernel, seq=seq, tc=tc, seg=seg, pitch=pitch)
    in_specs = [
        pl.BlockSpec((seq, LANES), lambda b, c: (row_block0 + b, COL_XR * nblk + c)),
        pl.BlockSpec((seq, LANES), lambda b, c: (row_block0 + b, COL_RGG * nblk + c)),
        pl.BlockSpec((CONV_W, LANES), lambda b, c: (0, c)),
        pl.BlockSpec((1, LANES), lambda b, c: (0, c)),
        pl.BlockSpec((None, LANES, 4 * LANES), lambda b, c: (c, 0, 0)),
        pl.BlockSpec((None, 1, 4 * LANES), lambda b, c: (c, 0, 0)),
        pl.BlockSpec((2, LANES), lambda b, c: (0, c)),
        pl.BlockSpec((None, 2, LANES), lambda b, c: (b, 0, c)),
    ]
    args = [p, p, cw, cb, wcat, bcat, lam, h0]
    out_shape = [jax.ShapeDtypeStruct((N_TOK, W_RG), BF16), jax.ShapeDtypeStruct((BATCH, 2, W_RG), F32)]
    out_specs = [pl.BlockSpec((seq, LANES), lambda b, c: (row_block0 + b, c)),
                 pl.BlockSpec((None, 2, LANES), lambda b, c: (b, 0, c))]
    aliases = {}
    if y_prev is not None:
        in_specs.append(pl.BlockSpec(memory_space=pl.ANY))
        args.append(y_prev)
        aliases = {len(args) - 1: 0}
        kern_fn = lambda *refs: kern(*refs[:8], *refs[9:])
    else:
        kern_fn = kern
    return pl.pallas_call(
        kern_fn,
        grid=(BATCH, nblk),
        in_specs=in_specs,
        out_specs=out_specs,
        out_shape=out_shape,
        scratch_shapes=[pltpu.VMEM((seq + 2 * CONV_PAD, LANES), F32)] + [pltpu.VMEM((rows, LANES), F32)] * 4,
        input_output_aliases=aliases,
        compiler_params=_cparams("arbitrary", "arbitrary"),
        name=f"rglru_{seq}",
    )(*args)


def _qkconv_kernel(x_ref, cw_ref, cb_ref, sc_ref, o_ref, *, tc):
    cw = cw_ref[...]
    for in_off, out_off, n in ((CONV_PAD, 0, CTX_LEN), (2 * CONV_PAD + CTX_LEN, CTX_LEN, SEQ)):
        for c in range(n // tc):
            r0 = in_off + c * tc - 2
            xc = cb_ref[...] + cw[0:1, :] * x_ref[r0:r0 + tc, :]
            for j in range(1, CONV_W):
                xc = xc + cw[j:j + 1, :] * x_ref[r0 + j:r0 + j + tc, :]
            y = xc * _sigmoid(xc) * sc_ref[...]
            o_ref[out_off + c * tc:out_off + (c + 1) * tc, :] = y.astype(o_ref.dtype)


def _qkconv(x, cw, cb, scale):
    cwid = 256
    n = x.shape[-1]
    return pl.pallas_call(
        functools.partial(_qkconv_kernel, tc=256),
        grid=(BATCH, n // cwid),
        in_specs=[pl.BlockSpec((None, SEQ_PADDED, cwid), lambda b, c: (b, 0, c)),
                  pl.BlockSpec((CONV_W, cwid), lambda b, c: (0, c)),
                  pl.BlockSpec((1, cwid), lambda b, c: (0, c)),
                  pl.BlockSpec((1, cwid), lambda b, c: (0, c))],
        out_specs=pl.BlockSpec((None, SEQ_ALL, cwid), lambda b, c: (b, 0, c)),
        out_shape=jax.ShapeDtypeStruct((BATCH, SEQ_ALL, n), BF16),
        compiler_params=_cparams("arbitrary", "arbitrary"),
        name="qk_conv",
    )(x, cw, cb, scale)


def _lane_cumsum(x, reverse):
    lane = lax.broadcasted_iota(jnp.int32, x.shape, 1)
    s = 1
    while s < LANES:
        if reverse:
            x = x + jnp.where(lane < LANES - s, pltpu.roll(x, LANES - s, axis=1), 0.0)
        else:
            x = x + jnp.where(lane >= s, pltpu.roll(x, s, axis=1), 0.0)
        s *= 2
    return x


def _mlstm_kernel(*refs, reverse, final):
    if final:
        q_ref, k_ref, v_ref, g_ref, gb_ref, hf_ref, o_ref, c_ref, n_ref, m_ref = refs
    else:
        q_ref, k_ref, v_ref, g_ref, gb_ref, o_ref, c_ref, n_ref, m_ref = refs
    d = 1 if reverse else 0

    @pl.when(pl.program_id(1) == 0)
    def _():
        c_ref[...] = jnp.zeros(c_ref.shape, F32)
        n_ref[...] = jnp.zeros(n_ref.shape, F32)
        m_ref[...] = jnp.zeros(m_ref.shape, F32)

    g = g_ref[...] + gb_ref[...]
    gt = g.T
    base = d * 2 * M_HEADS
    i_rows = gt[base:base + M_HEADS, :]
    f_rows = -_softplus(-gt[base + M_HEADS:base + 2 * M_HEADS, :])
    bcum = _lane_cumsum(f_rows, reverse)
    pad = jnp.zeros((M_CHUNK - 2 * M_HEADS, M_CHUNK), F32)
    cols = jnp.concatenate([i_rows, bcum, pad], axis=0).T
    r_rows = i_rows - bcum
    last = 0 if reverse else M_CHUNK - 1

    row_i = lax.broadcasted_iota(jnp.int32, (M_CHUNK, M_CHUNK), 0)
    col_i = lax.broadcasted_iota(jnp.int32, (M_CHUNK, M_CHUNK), 1)
    mask = (col_i >= row_i) if reverse else (col_i <= row_i)

    for h in range(M_HEADS):
        sl = slice(h * M_HEAD_DIM, (h + 1) * M_HEAD_DIM)
        qh = q_ref[:, sl]
        kh = k_ref[:, sl]
        vh = v_ref[:, sl]
        i_col = cols[:, h:h + 1]
        b_col = cols[:, M_HEADS + h:M_HEADS + h + 1]
        m_prev = m_ref[h:h + 1, 0:1]
        b_last = bcum[h:h + 1, last:last + 1]

        log_d = jnp.where(mask, b_col + r_rows[h:h + 1, :], -jnp.inf)
        log_inter = b_col + m_prev
        m_t = jnp.maximum(jnp.max(log_d, axis=-1, keepdims=True), log_inter)
        s = lax.dot_general(qh, kh, (((1,), (1,)), ((), ())), preferred_element_type=F32)
        s = s * jnp.exp(log_d - m_t)
        w_inter = jnp.exp(log_inter - m_t)
        c_prev = c_ref[h]
        n_prev = n_ref[h]
        qc = lax.dot_general(qh, c_prev.astype(BF16), (((1,), (1,)), ((), ())), preferred_element_type=F32)
        num = jnp.dot(s.astype(BF16), vh, preferred_element_type=F32) + w_inter * qc
        qn = jnp.sum(qh.astype(F32) * n_prev, axis=-1, keepdims=True)
        den = jnp.sum(s, axis=-1, keepdims=True) + w_inter * qn
        hh = num / jnp.maximum(jnp.abs(den), jnp.exp(-m_t))

        log_w = b_last - b_col + i_col
        m_new = jnp.maximum(b_last + m_prev, jnp.max(log_w, axis=0, keepdims=True))
        w = jnp.exp(log_w - m_new)
        decay = jnp.exp(b_last + m_prev - m_new)
        vw = (vh.astype(F32) * w).astype(BF16)
        upd = lax.dot_general(vw, kh, (((0,), (0,)), ((), ())), preferred_element_type=F32)
        c_ref[h] = decay * c_prev + upd
        n_ref[h] = decay * n_prev + jnp.sum(kh.astype(F32) * w, axis=0, keepdims=True)
        m_ref[h:h + 1, :] = jnp.broadcast_to(m_new, (1, LANES))

        if final:
            hs = hf_ref[:, sl] + hh
            o_ref[:, sl] = _layer_norm(hs)
        else:
            o_ref[:, sl] = hh


def _chunk_order(i, reverse):
    if not reverse:
        return i
    nctx = CTX_LEN // M_CHUNK
    ntot = SEQ_ALL // M_CHUNK
    return jnp.where(i < nctx, nctx - 1 - i, ntot + nctx - 1 - i)


def _mlstm(q_and_k, v, g, gate_b, h_fwd, *, reverse):
    final = h_fwd is not None
    nh = W_M // W_M
    blk = lambda col: pl.BlockSpec((None, M_CHUNK, W_M), lambda b, i: (b, _chunk_order(i, reverse), col))
    in_specs = [blk(0), blk(1), blk(0),
                pl.BlockSpec((None, M_CHUNK, LANES), lambda b, i: (b, _chunk_order(i, reverse), 0)),
                pl.BlockSpec((1, LANES), lambda b, i: (0, 0))]
    args = [q_and_k, q_and_k, v, g, gate_b]
    if final:
        in_specs.append(blk(0))
        args.append(h_fwd)
    del nh
    return pl.pallas_call(
        functools.partial(_mlstm_kernel, reverse=reverse, final=final),
        grid=(BATCH, SEQ_ALL // M_CHUNK),
        in_specs=in_specs,
        out_specs=blk(0),
        out_shape=jax.ShapeDtypeStruct((BATCH, SEQ_ALL, W_M), F32),
        scratch_shapes=[pltpu.VMEM((M_HEADS, M_HEAD_DIM, M_HEAD_DIM), F32),
                        pltpu.VMEM((M_HEADS, 1, M_HEAD_DIM), F32),
                        pltpu.VMEM((M_HEADS, LANES), F32)],
        compiler_params=_cparams("arbitrary", "arbitrary"),
        name="mlstm_bwd" if reverse else "mlstm_fwd",
    )(*args)


def _merge_kernel(yr_ref, hn_ref, o_ref, ga_ref, gb_ref, gn_ref, prg_ref, pm_ref, mix_ref):
    ym = (hn_ref[...] * gn_ref[...] * _sigmoid(o_ref[...])).astype(BF16)
    a = jnp.dot(yr_ref[...], prg_ref[...], preferred_element_type=F32)
    b = jnp.dot(ym, pm_ref[...], preferred_element_type=F32)
    mix_ref[...] = (_sigmoid(ga_ref[...]) * a + _sigmoid(gb_ref[...]) * b).astype(BF16)


def _merge(yr, hn, p, gn, p_rg, p_m, rows):
    tm, tn = 256, 1024
    nj = D_MODEL // tn
    return pl.pallas_call(
        _merge_kernel,
        grid=(nj, rows // tm),
        in_specs=[pl.BlockSpec((tm, W_RG), lambda j, i: (i, 0)),
                  pl.BlockSpec((tm, W_M), lambda j, i: (i, 0)),
                  pl.BlockSpec((tm, W_M), lambda j, i: (i, COL_O)),
                  pl.BlockSpec((tm, tn), lambda j, i: (i, COL_GA * nj + j)),
                  pl.BlockSpec((tm, tn), lambda j, i: (i, COL_GB * nj + j)),
                  pl.BlockSpec((1, W_M), lambda j, i: (0, 0)),
                  pl.BlockSpec((W_RG, tn), lambda j, i: (0, j)),
                  pl.BlockSpec((W_M, tn), lambda j, i: (0, j))],
        out_specs=pl.BlockSpec((tm, tn), lambda j, i: (i, j)),
        out_shape=jax.ShapeDtypeStruct((rows, D_MODEL), BF16),
        compiler_params=_cparams("arbitrary", "arbitrary"),
        name="merge",
    )(yr, hn, p, p, p, gn, p_rg, p_m)


def _res_ln(h, f, gate, lng, lnb):
    y = _layer_norm(ALPHA * h + gate * f)
    return y * lng + lnb


def _outproj_kernel(a_ref, w_ref, h_ref, g_ref, lng_ref, lnb_ref, o_ref):
    f = jnp.dot(a_ref[...], w_ref[...], preferred_element_type=F32)
    o_ref[...] = _res_ln(h_ref[...], f, g_ref[0], lng_ref[...], lnb_ref[...])


def _outproj(a, w, h, mod, gate_chunk, lng, lnb, rows):
    tm = 256
    return pl.pallas_call(
        _outproj_kernel,
        grid=(rows // tm,),
        in_specs=[pl.BlockSpec((tm, D_MODEL), lambda i: (i, 0)),
                  pl.BlockSpec((D_MODEL, D_MODEL), lambda i: (0, 0)),
                  pl.BlockSpec((tm, D_MODEL), lambda i: (i, 0)),
                  _mod_spec(gate_chunk, tm),
                  pl.BlockSpec((1, D_MODEL), lambda i: (0, 0)),
                  pl.BlockSpec((1, D_MODEL), lambda i: (0, 0))],
        out_specs=pl.BlockSpec((tm, D_MODEL), lambda i: (i, 0)),
        out_shape=jax.ShapeDtypeStruct((rows, D_MODEL), F32),
        compiler_params=_cparams("arbitrary"),
        name="out_proj",
    )(a, w, h, mod, lng, lnb)


def _resln_kernel(f_ref, h_ref, g_ref, lng_ref, lnb_ref, o_ref):
    o_ref[...] = _res_ln(h_ref[...], f_ref[...], g_ref[0], lng_ref[...], lnb_ref[...])


def _resln(f, h, mod, gate_chunk, lng, lnb):
    rows = f.shape[0]
    tm = 512
    return pl.pallas_call(
        _resln_kernel,
        grid=(rows // tm,),
        in_specs=[pl.BlockSpec((tm, D_MODEL), lambda i: (i, 0)),
                  pl.BlockSpec((tm, D_MODEL), lambda i: (i, 0)),
                  _mod_spec(gate_chunk, tm),
                  pl.BlockSpec((1, D_MODEL), lambda i: (0, 0)),
                  pl.BlockSpec((1, D_MODEL), lambda i: (0, 0))],
        out_specs=pl.BlockSpec((tm, D_MODEL), lambda i: (i, 0)),
        out_shape=jax.ShapeDtypeStruct((rows, D_MODEL), F32),
        compiler_params=_cparams("arbitrary"),
        name="res_ln",
    )(f, h, mod, lng, lnb)


def _ffn_kernel(h_ref, sh_ref, sc_ref, g_ref, w1_ref, w3_ref, w2_ref, lng_ref, lnb_ref, o_ref, u_ref, acc_ref):
    f = pl.program_id(1)

    @pl.when(f == 0)
    def _():
        u = _layer_norm(h_ref[...]) * (1.0 + sc_ref[0]) + sh_ref[0]
        u_ref[...] = u.astype(BF16)
        acc_ref[...] = jnp.zeros(acc_ref.shape, F32)

    u = u_ref[...]
    h1 = jnp.dot(u, w1_ref[...], preferred_element_type=F32)
    h3 = jnp.dot(u, w3_ref[...], preferred_element_type=F32)
    act = (h1 * _sigmoid(h1) * h3).astype(BF16)
    acc_ref[...] += jnp.dot(act, w2_ref[...], preferred_element_type=F32)

    @pl.when(f == pl.num_programs(1) - 1)
    def _():
        o_ref[...] = _res_ln(h_ref[...], acc_ref[...], g_ref[0], lng_ref[...], lnb_ref[...])


def _ffn(h, mod, w1, w3, w2, lng, lnb):
    rows = h.shape[0]
    tm, tf = 512, 512
    return pl.pallas_call(
        _ffn_kernel,
        grid=(rows // tm, D_FF // tf),
        in_specs=[pl.BlockSpec((tm, D_MODEL), lambda i, f: (i, 0)),
                  _mod_spec(3, tm), _mod_spec(4, tm), _mod_spec(5, tm),
                  pl.BlockSpec((D_MODEL, tf), lambda i, f: (0, f)),
                  pl.BlockSpec((D_MODEL, tf), lambda i, f: (0, f)),
                  pl.BlockSpec((tf, D_MODEL), lambda i, f: (f, 0)),
                  pl.BlockSpec((1, D_MODEL), lambda i, f: (0, 0)),
                  pl.BlockSpec((1, D_MODEL), lambda i, f: (0, 0))],
        out_specs=pl.BlockSpec((tm, D_MODEL), lambda i, f: (i, 0)),
        out_shape=jax.ShapeDtypeStruct((rows, D_MODEL), F32),
        scratch_shapes=[pltpu.VMEM((tm, D_MODEL), BF16), pltpu.VMEM((tm, D_MODEL), F32)],
        compiler_params=_cparams("arbitrary", "arbitrary"),
        name="ffn_dense",
    )(h, mod, mod, mod, w1, w3, w2, lng, lnb)


def _route_kernel(h_ref, sh_ref, sc_ref, rw_ref, t_ref, lg_ref):
    u = (_layer_norm(h_ref[...]) * (1.0 + sc_ref[0]) + sh_ref[0]).astype(BF16)
    t_ref[...] = u
    lg_ref[...] = jnp.dot(u, rw_ref[...], preferred_element_type=F32)


def _route(h, mod, rw):
    rows = h.shape[0]
    tm = 512
    return pl.pallas_call(
        _route_kernel,
        grid=(rows // tm,),
        in_specs=[pl.BlockSpec((tm, D_MODEL), lambda i: (i, 0)),
                  _mod_spec(3, tm), _mod_spec(4, tm),
                  pl.BlockSpec((D_MODEL, LANES), lambda i: (0, 0))],
        out_specs=[pl.BlockSpec((tm, D_MODEL), lambda i: (i, 0)),
                   pl.BlockSpec((tm, LANES), lambda i: (i, 0))],
        out_shape=[jax.ShapeDtypeStruct((rows, D_MODEL), BF16), jax.ShapeDtypeStruct((rows, LANES), F32)],
        compiler_params=_cparams("arbitrary"),
        name="moe_route",
    )(h, mod, mod, rw)


def _experts_kernel(te_ref, tv_ref, x_ref, w1_ref, w3_ref, w2_ref, o_ref, acc_ref):
    i = pl.program_id(0)
    f = pl.program_id(1)
    live = tv_ref[i] > 0

    @pl.when(jnp.logical_and(live, f == 0))
    def _():
        acc_ref[...] = jnp.zeros(acc_ref.shape, F32)

    @pl.when(live)
    def _():
        x = x_ref[...]
        h1 = jnp.dot(x, w1_ref[...], preferred_element_type=F32)
        h3 = jnp.dot(x, w3_ref[...], preferred_element_type=F32)
        act = (h1 * _sigmoid(h1) * h3).astype(BF16)
        acc_ref[...] += jnp.dot(act, w2_ref[...], preferred_element_type=F32)

    @pl.when(f == pl.num_programs(1) - 1)
    def _():
        o_ref[...] = jnp.where(live, acc_ref[...], 0.0)


def _experts(tile_expert, tile_live, xb, w1, w3, w2):
    tf = 512
    grid_spec = pltpu.PrefetchScalarGridSpec(
        num_scalar_prefetch=2,
        grid=(MOE_TILES, D_FF // tf),
        in_specs=[pl.BlockSpec((MOE_TILE, D_MODEL), lambda i, f, te, tv: (i, 0)),
                  pl.BlockSpec((None, D_MODEL, tf), lambda i, f, te, tv: (te[i], 0, f)),
                  pl.BlockSpec((None, D_MODEL, tf), lambda i, f, te, tv: (te[i], 0, f)),
                  pl.BlockSpec((None, tf, D_MODEL), lambda i, f, te, tv: (te[i], f, 0))],
        out_specs=pl.BlockSpec((MOE_TILE, D_MODEL), lambda i, f, te, tv: (i, 0)),
        scratch_shapes=[pltpu.VMEM((MOE_TILE, D_MODEL), F32)],
    )
    return pl.pallas_call(
        _experts_kernel,
        grid_spec=grid_spec,
        out_shape=jax.ShapeDtypeStruct((MOE_TILES * MOE_TILE, D_MODEL), F32),
        compiler_params=_cparams("arbitrary", "arbitrary"),
        name="moe_experts",
    )(tile_expert, tile_live, xb, w1, w3, w2)


def _moe(h, mod, router_w, router_b, w1, w3, w2):
    n = h.shape[0]
    rw = jnp.zeros((D_MODEL, LANES), BF16).at[:, :N_EXPERTS].set(router_w.astype(BF16))
    t, lg = _route(h, mod, rw)
    logits = lg[:, :N_EXPERTS] + router_b
    top_val, top_idx = lax.top_k(logits, TOP_K)
    gates = jax.nn.softmax(top_val, axis=-1)
    expert = top_idx.reshape(-1)
    token = jnp.repeat(jnp.arange(n), TOP_K)
    weight = gates.reshape(-1)
    n_assign = n * TOP_K
    order = jnp.argsort(expert)
    e_s, tok_s, w_s = expert[order], token[order], weight[order]
    counts = jnp.bincount(expert, length=N_EXPERTS)
    padded = (counts + MOE_TILE - 1) // MOE_TILE * MOE_TILE
    pend = jnp.cumsum(padded)
    pstart = pend - padded
    sstart = jnp.cumsum(counts) - counts
    pos = pstart[e_s] + jnp.arange(n_assign) - sstart[e_s]
    xb = jnp.zeros((MOE_TILES * MOE_TILE, D_MODEL), BF16).at[pos].set(t[tok_s])
    tile_start = jnp.arange(MOE_TILES) * MOE_TILE
    tile_expert = jnp.minimum(jnp.searchsorted(pend, tile_start, side='right'), N_EXPERTS - 1).astype(jnp.int32)
    tile_live = (tile_start < pend[-1]).astype(jnp.int32)
    yb = _experts(tile_expert, tile_live, xb, w1, w3, w2)
    return jnp.zeros((n, D_MODEL), F32).at[tok_s].add(yb[pos] * w_s[:, None])


def _to_colmajor(t):
    b, l, c = t.shape
    return t.reshape(b, l // GRID_W, GRID_W, c).transpose(0, 2, 1, 3).reshape(b, l, c)


def _seq_layout(p_cols, pad):
    c = p_cols.shape[-1]
    lat = _to_colmajor(p_cols[:N_LAT].reshape(BATCH, SEQ, c))
    ctx = p_cols[N_LAT:].reshape(BATCH, CTX_LEN, c)
    if not pad:
        return jnp.concatenate([ctx, lat], axis=1)
    z = jnp.zeros((BATCH, CONV_PAD, c), p_cols.dtype)
    return jnp.concatenate([z, ctx, z, lat, z], axis=1)


def _token_layout(s):
    c = s.shape[-1]
    lat = s[:, CTX_LEN:].reshape(BATCH, GRID_W, SEQ // GRID_W, c).transpose(0, 2, 1, 3).reshape(N_LAT, c)
    ctx = s[:, :CTX_LEN].reshape(N_CTX, c)
    return jnp.concatenate([lat, ctx], axis=0)


def _mixer(h, mod, w_in, conv_rg_w, conv_rg_b, conv_m_w, conv_m_b, rg_wa, rg_ba, rg_wx, rg_bx, rg_lam,
           m_gate_b, m_gn_g, p_rg, p_m, rows_out):
    gate0 = C_M + 4 * W_M
    w_main = jnp.concatenate([w_in[:, :gate0], w_in[:, gate0 + N_GATES:]], axis=1).astype(BF16)
    w_gate = jnp.zeros((D_MODEL, LANES), BF16).at[:, :N_GATES].set(w_in[:, gate0:gate0 + N_GATES].astype(BF16))
    p = _proj(h, mod, w_main, 512, 1024)
    pg = _proj(h, mod, w_gate, 512, LANES)

    wcat = jnp.concatenate([rg_wa[0], rg_wx[0], rg_wa[1], rg_wx[1]], axis=-1).astype(BF16)
    bcat = jnp.stack([rg_ba[0], rg_bx[0], rg_ba[1], rg_bx[1]], axis=0)
    bcat = bcat.reshape(4, RG_BLOCKS, RG_BLOCK).transpose(1, 0, 2).reshape(RG_BLOCKS, 1, 4 * RG_BLOCK)
    cb = conv_rg_b.reshape(1, W_RG)
    h0 = jnp.zeros((BATCH, 2, W_RG), F32)
    yr, st = _rg_branch(p, None, h0, conv_rg_w, cb, wcat, bcat, rg_lam, seq=CTX_LEN, row_block0=N_LAT // CTX_LEN)
    yr, _ = _rg_branch(p, yr, st, conv_rg_w, cb, wcat, bcat, rg_lam, seq=SEQ, row_block0=0)

    qk_in = _seq_layout(p[:, COL_Q * D_MODEL:(COL_K + 1) * D_MODEL], True)
    v_seq = _seq_layout(p[:, COL_V * D_MODEL:(COL_V + 1) * D_MODEL].astype(BF16), False)
    g_seq = _seq_layout(pg, False)
    qscale = jnp.concatenate([jnp.full((1, W_M), M_HEAD_DIM ** -0.5, F32), jnp.ones((1, W_M), F32)], axis=1)
    qk = _qkconv(qk_in, conv_m_w, conv_m_b.reshape(1, 2 * W_M), qscale)
    gate_b = jnp.zeros((1, LANES), F32).at[0, :N_GATES].set(m_gate_b.reshape(-1))
    h_f = _mlstm(qk, v_seq, g_seq, gate_b, None, reverse=False)
    hn_seq = _mlstm(qk, v_seq, g_seq, gate_b, h_f, reverse=True)
    hn = _token_layout(hn_seq)

    return _merge(yr, hn, p, m_gn_g.reshape(1, W_M), p_rg.astype(BF16), p_m.astype(BF16), rows_out)


def kernel(x, c, ctx, c_ctx, w_mod, b_mod, w_in, conv_rg_w, conv_rg_b, conv_m_w, conv_m_b, rg_wa, rg_ba, rg_wx, rg_bx, rg_lam, m_gate_b, m_gn_g, p_rg, p_m, w_out, ln_g, ln_b, ff_w1, ff_w3, ff_w2, router_w, router_b, ex_w1, ex_w3, ex_w2):
    h = jnp.concatenate([x.reshape(N_LAT, D_MODEL), ctx.reshape(N_CTX, D_MODEL)], axis=0)
    cond = jnp.zeros((SUBLANES, D_MODEL), F32).at[:BATCH].set(c).at[BATCH].set(c_ctx)
    for l in range(DEPTH):
        last = l == DEPTH - 1
        rows = N_LAT if last else N_TOK
        mod = _modvec(cond, w_mod[l], b_mod[l])
        mix = _mixer(h, mod, w_in[l], conv_rg_w[l], conv_rg_b[l], conv_m_w[l], conv_m_b[l], rg_wa[l], rg_ba[l],
                     rg_wx[l], rg_bx[l], rg_lam[l], m_gate_b[l], m_gn_g[l], p_rg[l], p_m[l], rows)
        lng = ln_g[l].reshape(2, 1, D_MODEL)
        lnb = ln_b[l].reshape(2, 1, D_MODEL)
        h = _outproj(mix, w_out[l].astype(BF16), h, mod, 2, lng[0], lnb[0], rows)
        if l % 2 == 0:
            k = l // 2
            h = _ffn(h, mod, ff_w1[k].astype(BF16), ff_w3[k].astype(BF16), ff_w2[k].astype(BF16), lng[1], lnb[1])
        else:
            k = l // 2
            f = _moe(h, mod, router_w[k], router_b[k], ex_w1[k].astype(BF16), ex_w3[k].astype(BF16),
                     ex_w2[k].astype(BF16))
            h = _resln(f, h, mod, 5, lng[1], lnb[1])
    return h[:N_LAT].reshape(BATCH, SEQ, D_MODEL)
```

```python
import functools

import jax
import jax.numpy as jnp
from jax import lax
from jax.experimental import pallas as pl
from jax.experimental.pallas import tpu as pltpu

F32 = jnp.float32
BF16 = jnp.bfloat16

D_MODEL = 2048
BATCH = 4
SEQ = 4096
DEPTH = 2
GRID_W = 64
CTX_LEN = 256
W_RG = 2048
RG_BLOCKS = 16
RG_BLOCK = W_RG // RG_BLOCKS
RG_C = 8.0
CONV_W = 4
M_HEADS = 8
M_HEAD_DIM = 256
W_M = M_HEADS * M_HEAD_DIM
M_CHUNK = 128
C_M = 2 * W_RG
N_GATES = 4 * M_HEADS
D_FF = 7168
N_EXPERTS = 8
TOP_K = 2
ALPHA = (2.0 * DEPTH) ** 0.25
LN_EPS = 1e-6

N_LAT = BATCH * SEQ
N_CTX = BATCH * CTX_LEN
N_TOK = N_LAT + N_CTX
LANES = 128
SUBLANES = 8
VMEM_LIMIT = 52 * 1024 * 1024

COL_XR, COL_RGG, COL_Q, COL_K, COL_V, COL_O = range(6)

SEQ_ALL = CTX_LEN + SEQ
CONV_PAD = 8
SEQ_PADDED = CONV_PAD + CTX_LEN + CONV_PAD + SEQ + CONV_PAD
MOE_TILE = 512
MOE_TILES = (N_LAT * TOP_K) // MOE_TILE + N_EXPERTS


def _cparams(*sem):
    return pltpu.CompilerParams(dimension_semantics=sem, vmem_limit_bytes=VMEM_LIMIT)


def _layer_norm(x):
    mu = jnp.mean(x, axis=-1, keepdims=True)
    xc = x - mu
    var = jnp.mean(xc * xc, axis=-1, keepdims=True)
    return xc * lax.rsqrt(var + LN_EPS)


def _sigmoid(x):
    return 0.5 * jnp.tanh(0.5 * x) + 0.5


def _softplus(x):
    return jnp.maximum(x, 0.0) + jnp.log1p(jnp.exp(-jnp.abs(x)))


def _mod_row(i, tm):
    return jnp.where(i >= N_LAT // tm, BATCH, i // (SEQ // tm))


def _mod_spec(chunk, tm):
    return pl.BlockSpec((1, 1, D_MODEL), lambda i, *_: (_mod_row(i, tm), 0, chunk))


def _modvec_kernel(s_ref, w_ref, b_ref, o_ref):
    s = s_ref[...]
    s = s * _sigmoid(s)
    o_ref[...] = jnp.dot(s.astype(BF16), w_ref[...].astype(BF16), preferred_element_type=F32) + b_ref[...]


def _modvec(cond, w, b):
    n = w.shape[1]
    tn = 1024
    out = pl.pallas_call(
        _modvec_kernel,
        grid=(n // tn,),
        in_specs=[pl.BlockSpec((SUBLANES, D_MODEL), lambda j: (0, 0)),
                  pl.BlockSpec((D_MODEL, tn), lambda j: (0, j)),
                  pl.BlockSpec((1, tn), lambda j: (0, j))],
        out_specs=pl.BlockSpec((SUBLANES, tn), lambda j: (0, j)),
        out_shape=jax.ShapeDtypeStruct((SUBLANES, n), F32),
        compiler_params=_cparams("arbitrary"),
        name="modvec",
    )(cond, w, b.reshape(1, n))
    return out.reshape(SUBLANES, 1, n)


def _lnmod_kernel(x_ref, sh_ref, sc_ref, wg_ref, u_ref, g_ref):
    u = (_layer_norm(x_ref[...]) * (1.0 + sc_ref[0]) + sh_ref[0]).astype(BF16)
    u_ref[...] = u
    g_ref[...] = jnp.dot(u, wg_ref[...], preferred_element_type=F32)


def _lnmod(h, mod, wg):
    r = h.shape[0]
    tm = 512
    return pl.pallas_call(
        _lnmod_kernel,
        grid=(r // tm,),
        in_specs=[pl.BlockSpec((tm, D_MODEL), lambda i: (i, 0)),
                  _mod_spec(0, tm), _mod_spec(1, tm),
                  pl.BlockSpec((D_MODEL, LANES), lambda i: (0, 0))],
        out_specs=[pl.BlockSpec((tm, D_MODEL), lambda i: (i, 0)),
                   pl.BlockSpec((tm, LANES), lambda i: (i, 0))],
        out_shape=[jax.ShapeDtypeStruct((r, D_MODEL), BF16), jax.ShapeDtypeStruct((r, LANES), F32)],
        compiler_params=_cparams("arbitrary"),
        name="ln_mod",
    )(h, mod, mod, wg)


def _proj_kernel(u_ref, w_ref, o_ref, wb_ref):
    @pl.when(pl.program_id(1) == 0)
    def _():
        wb_ref[...] = w_ref[...].astype(BF16)

    o_ref[...] = jnp.dot(u_ref[...], wb_ref[...], preferred_element_type=F32)


def _proj(u, w, n):
    r = u.shape[0]
    tm, tn = 1024, 1024
    return pl.pallas_call(
        _proj_kernel,
        grid=(n // tn, r // tm),
        in_specs=[pl.BlockSpec((tm, D_MODEL), lambda j, i: (i, 0)),
                  pl.BlockSpec((D_MODEL, tn), lambda j, i: (0, j))],
        out_specs=pl.BlockSpec((tm, tn), lambda j, i: (i, j)),
        out_shape=jax.ShapeDtypeStruct((r, n), F32),
        scratch_shapes=[pltpu.VMEM((D_MODEL, tn), BF16)],
        compiler_params=_cparams("arbitrary", "arbitrary"),
        name="proj_in",
    )(u, w)


def _rg_kernel(x_ref, g_ref, cw_ref, cb_ref, w_ref, bias_ref, lam_ref, h0_ref, y_ref, hl_ref,
               xs, a0, b0, a1, b1, *, seq, tc, seg, pitch):
    zeros8 = jnp.zeros((CONV_PAD, LANES), F32)
    xs[0:CONV_PAD, :] = zeros8
    xs[CONV_PAD + seq:CONV_PAD + seq + CONV_PAD, :] = zeros8
    xs[CONV_PAD:CONV_PAD + seq, :] = x_ref[...]

    cw = cw_ref[...]
    sp = _softplus(-lam_ref[...])
    dec = ((a0, b0), (a1, b1))
    for c in range(seq // tc):
        r0 = c * tc
        xc = cb_ref[...] + cw[0:1, :] * xs[r0 + CONV_PAD - 2:r0 + CONV_PAD - 2 + tc, :]
        for j in range(1, CONV_W):
            xc = xc + cw[j:j + 1, :] * xs[r0 + CONV_PAD - 2 + j:r0 + CONV_PAD - 2 + j + tc, :]
        z = jnp.dot(xc.astype(BF16), w_ref[...], preferred_element_type=F32) + bias_ref[...]
        for d in range(2):
            rg = _sigmoid(z[:, (2 * d) * LANES:(2 * d + 1) * LANES])
            ig = _sigmoid(z[:, (2 * d + 1) * LANES:(2 * d + 2) * LANES])
            log_a = (-RG_C) * rg * sp[d:d + 1, :]
            a = jnp.exp(log_a)
            b = jnp.sqrt(jnp.tanh(-log_a) * (1.0 + a * a)) * (ig * xc)
            a_ref, b_ref = dec[d]
            for p in range(max(tc // seg, 1)):
                n = min(tc, seg)
                t = r0 + p * n
                dst = (t // seg) * pitch + t % seg
                a_ref[dst:dst + n, :] = a[p * n:(p + 1) * n, :]
                b_ref[dst:dst + n, :] = b[p * n:(p + 1) * n, :]

    def scan_step(i, carry):
        hf, pf, hb, pb = carry
        tf = i
        tb = seg - 1 - i
        af = a0[pl.ds(tf, SUBLANES, stride=pitch), :]
        bf = b0[pl.ds(tf, SUBLANES, stride=pitch), :]
        ab = a1[pl.ds(tb, SUBLANES, stride=pitch), :]
        bb = b1[pl.ds(tb, SUBLANES, stride=pitch), :]
        hf = af * hf + bf
        pf = pf * af
        hb = ab * hb + bb
        pb = pb * ab
        b0[pl.ds(tf, SUBLANES, stride=pitch), :] = hf
        a0[pl.ds(tf, SUBLANES, stride=pitch), :] = pf
        b1[pl.ds(tb, SUBLANES, stride=pitch), :] = hb
        a1[pl.ds(tb, SUBLANES, stride=pitch), :] = pb
        return hf, pf, hb, pb

    z8 = jnp.zeros((SUBLANES, LANES), F32)
    o8 = jnp.ones((SUBLANES, LANES), F32)
    hf, pf, hb, pb = lax.fori_loop(0, seg, scan_step, (z8, o8, z8, o8), unroll=4)

    h0 = h0_ref[...]
    cf = [None] * SUBLANES
    cbk = [None] * SUBLANES
    carry = h0[0:1, :]
    for s in range(SUBLANES):
        cf[s] = carry
        carry = pf[s:s + 1, :] * carry + hf[s:s + 1, :]
    hl_f = carry
    carry = h0[1:2, :]
    for s in reversed(range(SUBLANES)):
        cbk[s] = carry
        carry = pb[s:s + 1, :] * carry + hb[s:s + 1, :]
    hl_b = carry
    hl_ref[...] = jnp.concatenate([hl_f, hl_b], axis=0)

    for s in range(SUBLANES):
        for c in range(max(seg // tc, 1)):
            n = min(tc, seg)
            src = s * pitch + c * n
            t = s * seg + c * n
            h = (b0[src:src + n, :] + a0[src:src + n, :] * cf[s]
                 + b1[src:src + n, :] + a1[src:src + n, :] * cbk[s])
            y_ref[t:t + n, :] = (h * jax.nn.gelu(g_ref[t:t + n, :])).astype(y_ref.dtype)


def _rg_branch(p, h0, cw, cb, wcat, bcat, lam, *, seq, row_block0):
    tc = min(256, seq)
    seg = seq // SUBLANES
    pitch = seg + SUBLANES if (seg // SUBLANES) % 2 == 0 else seg
    nblk = W_RG // LANES
    rows = SUBLANES * pitch
    kern = functools.partial(_rg_kernel, seq=seq, tc=tc, seg=seg, pitch=pitch)
    in_specs = [
        pl.BlockSpec((seq, LANES), lambda b, c: (row_block0 + b, COL_XR * nblk + c)),
        pl.BlockSpec((seq, LANES), lambda b, c: (row_block0 + b, COL_RGG * nblk + c)),
        pl.BlockSpec((CONV_W, LANES), lambda b, c: (0, c)),
        pl.BlockSpec((1, LANES), lambda b, c: (0, c)),
        pl.BlockSpec((None, LANES, 4 * LANES), lambda b, c: (c, 0, 0)),
        pl.BlockSpec((None, 1, 4 * LANES), lambda b, c: (c, 0, 0)),
        pl.BlockSpec((2, LANES), lambda b, c: (0, c)),
        pl.BlockSpec((None, 2, LANES), lambda b, c: (b, 0, c)),
    ]
    out_shape = [jax.ShapeDtypeStruct((BATCH * seq, W_RG), BF16), jax.ShapeDtypeStruct((BATCH, 2, W_RG), F32)]
    out_specs = [pl.BlockSpec((seq, LANES), lambda b, c: (b, c)),
                 pl.BlockSpec((None, 2, LANES), lambda b, c: (b, 0, c))]
    return pl.pallas_call(
        kern,
        grid=(BATCH, nblk),
        in_specs=in_specs,
        out_specs=out_specs,
        out_shape=out_shape,
        scratch_shapes=[pltpu.VMEM((seq + 2 * CONV_PAD, LANES), F32)] + [pltpu.VMEM((rows, LANES), F32)] * 4,
        compiler_params=_cparams("arbitrary", "arbitrary"),
        name=f"rglru_{seq}",
    )(p, p, cw, cb, wcat, bcat, lam, h0)


def _qkconv_kernel(x_ref, cw_ref, cb_ref, sc_ref, o_ref, *, tc):
    cw = cw_ref[...]
    for in_off, out_off, n in ((CONV_PAD, 0, CTX_LEN), (2 * CONV_PAD + CTX_LEN, CTX_LEN, SEQ)):
        for c in range(n // tc):
            r0 = in_off + c * tc - 2
            xc = cb_ref[...] + cw[0:1, :] * x_ref[r0:r0 + tc, :]
            for j in range(1, CONV_W):
                xc = xc + cw[j:j + 1, :] * x_ref[r0 + j:r0 + j + tc, :]
            y = xc * _sigmoid(xc) * sc_ref[...]
            o_ref[out_off + c * tc:out_off + (c + 1) * tc, :] = y.astype(o_ref.dtype)


def _qkconv(x, cw, cb, scale):
    cwid = 256
    n = x.shape[-1]
    return pl.pallas_call(
        functools.partial(_qkconv_kernel, tc=256),
        grid=(BATCH, n // cwid),
        in_specs=[pl.BlockSpec((None, SEQ_PADDED, cwid), lambda b, c: (b, 0, c)),
                  pl.BlockSpec((CONV_W, cwid), lambda b, c: (0, c)),
                  pl.BlockSpec((1, cwid), lambda b, c: (0, c)),
                  pl.BlockSpec((1, cwid), lambda b, c: (0, c))],
        out_specs=pl.BlockSpec((None, SEQ_ALL, cwid), lambda b, c: (b, 0, c)),
        out_shape=jax.ShapeDtypeStruct((BATCH, SEQ_ALL, n), BF16),
        compiler_params=_cparams("arbitrary", "arbitrary"),
        name="qk_conv",
    )(x, cw, cb, scale)


def _lane_cumsum(x, reverse):
    lane = lax.broadcasted_iota(jnp.int32, x.shape, 1)
    s = 1
    while s < LANES:
        if reverse:
            x = x + jnp.where(lane < LANES - s, pltpu.roll(x, LANES - s, axis=1), 0.0)
        else:
            x = x + jnp.where(lane >= s, pltpu.roll(x, s, axis=1), 0.0)
        s *= 2
    return x


def _mlstm_direction(q_ref, k_ref, v_ref, g, o_ref, c_ref, n_ref, m_ref, d):
    reverse = d == 1
    gt = g.T
    base = d * 2 * M_HEADS
    i_rows = gt[base:base + M_HEADS, :]
    f_rows = -_softplus(-gt[base + M_HEADS:base + 2 * M_HEADS, :])
    bcum = _lane_cumsum(f_rows, reverse)
    pad = jnp.zeros((M_CHUNK - 2 * M_HEADS, M_CHUNK), F32)
    cols = jnp.concatenate([i_rows, bcum, pad], axis=0).T
    r_rows = i_rows - bcum
    last = 0 if reverse else M_CHUNK - 1

    row_i = lax.broadcasted_iota(jnp.int32, (M_CHUNK, M_CHUNK), 0)
    col_i = lax.broadcasted_iota(jnp.int32, (M_CHUNK, M_CHUNK), 1)
    mask = (col_i >= row_i) if reverse else (col_i <= row_i)

    for h in range(M_HEADS):
        sl = slice(h * M_HEAD_DIM, (h + 1) * M_HEAD_DIM)
        qh = q_ref[:, sl]
        kh = k_ref[:, sl]
        vh = v_ref[:, sl]
        i_col = cols[:, h:h + 1]
        b_col = cols[:, M_HEADS + h:M_HEADS + h + 1]
        m_prev = m_ref[d, h:h + 1, 0:1]
        b_last = bcum[h:h + 1, last:last + 1]

        log_d = jnp.where(mask, b_col + r_rows[h:h + 1, :], -jnp.inf)
        log_inter = b_col + m_prev
        m_t = jnp.maximum(jnp.max(log_d, axis=-1, keepdims=True), log_inter)
        s = lax.dot_general(qh, kh, (((1,), (1,)), ((), ())), preferred_element_type=F32)
        s = s * jnp.exp(log_d - m_t)
        w_inter = jnp.exp(log_inter - m_t)
        c_prev = c_ref[d, h]
        n_prev = n_ref[d, h]
        qc = lax.dot_general(qh, c_prev.astype(BF16), (((1,), (1,)), ((), ())), preferred_element_type=F32)
        num = jnp.dot(s.astype(BF16), vh, preferred_element_type=F32) + w_inter * qc
        qn = jnp.sum(qh.astype(F32) * n_prev, axis=-1, keepdims=True)
        den = jnp.sum(s, axis=-1, keepdims=True) + w_inter * qn
        o_ref[:, sl] = num / jnp.maximum(jnp.abs(den), jnp.exp(-m_t))

        log_w = b_last - b_col + i_col
        m_new = jnp.maximum(b_last + m_prev, jnp.max(log_w, axis=0, keepdims=True))
        w = jnp.exp(log_w - m_new)
        decay = jnp.exp(b_last + m_prev - m_new)
        vw = (vh.astype(F32) * w).astype(BF16)
        upd = lax.dot_general(vw, kh, (((0,), (0,)), ((), ())), preferred_element_type=F32)
        c_ref[d, h] = decay * c_prev + upd
        n_ref[d, h] = decay * n_prev + jnp.sum(kh.astype(F32) * w, axis=0, keepdims=True)
        m_ref[d, h:h + 1, :] = jnp.broadcast_to(m_new, (1, LANES))


def _mlstm_kernel(qf_ref, kf_ref, vf_ref, gf_ref, qb_ref, kb_ref, vb_ref, gb_ref, bias_ref, of_ref, ob_ref,
                  c_ref, n_ref, m_ref):
    @pl.when(pl.program_id(1) == 0)
    def _():
        c_ref[...] = jnp.zeros(c_ref.shape, F32)
        n_ref[...] = jnp.zeros(n_ref.shape, F32)
        m_ref[...] = jnp.zeros(m_ref.shape, F32)

    bias = bias_ref[...]
    _mlstm_direction(qf_ref, kf_ref, vf_ref, gf_ref[...] + bias, of_ref, c_ref, n_ref, m_ref, 0)
    _mlstm_direction(qb_ref, kb_ref, vb_ref, gb_ref[...] + bias, ob_ref, c_ref, n_ref, m_ref, 1)


def _chunk_order(i, reverse):
    if not reverse:
        return i
    nctx = CTX_LEN // M_CHUNK
    ntot = SEQ_ALL // M_CHUNK
    return jnp.where(i < nctx, nctx - 1 - i, ntot + nctx - 1 - i)


def _mlstm(q_and_k, v, g, gate_b):
    def blk(col, reverse, width=W_M):
        return pl.BlockSpec((None, M_CHUNK, width), lambda b, i: (b, _chunk_order(i, reverse), col))

    in_specs = [blk(0, False), blk(1, False), blk(0, False), blk(0, False, LANES),
                blk(0, True), blk(1, True), blk(0, True), blk(0, True, LANES),
                pl.BlockSpec((1, LANES), lambda b, i: (0, 0))]
    out = jax.ShapeDtypeStruct((BATCH, SEQ_ALL, W_M), F32)
    return pl.pallas_call(
        _mlstm_kernel,
        grid=(BATCH, SEQ_ALL // M_CHUNK),
        in_specs=in_specs,
        out_specs=[blk(0, False), blk(0, True)],
        out_shape=[out, out],
        scratch_shapes=[pltpu.VMEM((2, M_HEADS, M_HEAD_DIM, M_HEAD_DIM), F32),
                        pltpu.VMEM((2, M_HEADS, 1, M_HEAD_DIM), F32),
                        pltpu.VMEM((2, M_HEADS, LANES), F32)],
        compiler_params=_cparams("arbitrary", "arbitrary"),
        name="mlstm",
    )(q_and_k, q_and_k, v, g, q_and_k, q_and_k, v, g, gate_b)


def _merge_kernel(yr_ref, hs_ref, o_ref, ga_ref, gb_ref, gn_ref, prg_ref, pm_ref, mix_ref):
    hs = hs_ref[...]
    hn = jnp.concatenate([_layer_norm(hs[:, h * M_HEAD_DIM:(h + 1) * M_HEAD_DIM]) for h in range(M_HEADS)], axis=1)
    ym = (hn * gn_ref[...] * _sigmoid(o_ref[...])).astype(BF16)
    a = jnp.dot(yr_ref[...], prg_ref[...], preferred_element_type=F32)
    b = jnp.dot(ym, pm_ref[...], preferred_element_type=F32)
    mix_ref[...] = (_sigmoid(ga_ref[...]) * a + _sigmoid(gb_ref[...]) * b).astype(BF16)


def _merge(yr, hs, p1, p2, gn, p_rg, p_m, rows):
    tm, tn = 256, 1024
    nj = D_MODEL // tn
    return pl.pallas_call(
        _merge_kernel,
        grid=(nj, rows // tm),
        in_specs=[pl.BlockSpec((tm, W_RG), lambda j, i: (i, 0)),
                  pl.BlockSpec((tm, W_M), lambda j, i: (i, 0)),
                  pl.BlockSpec((tm, W_M), lambda j, i: (i, COL_O)),
                  pl.BlockSpec((tm, tn), lambda j, i: (i, j)),
                  pl.BlockSpec((tm, tn), lambda j, i: (i, nj + j)),
                  pl.BlockSpec((1, W_M), lambda j, i: (0, 0)),
                  pl.BlockSpec((W_RG, tn), lambda j, i: (0, j)),
                  pl.BlockSpec((W_M, tn), lambda j, i: (0, j))],
        out_specs=pl.BlockSpec((tm, tn), lambda j, i: (i, j)),
        out_shape=jax.ShapeDtypeStruct((rows, D_MODEL), BF16),
        compiler_params=_cparams("arbitrary", "arbitrary"),
        name="merge",
    )(yr, hs, p1, p2, p2, gn, p_rg, p_m)


def _res_ln(h, f, gate, lng, lnb):
    y = _layer_norm(ALPHA * h + gate * f)
    return y * lng + lnb


def _outproj_kernel(a_ref, w_ref, h_ref, g_ref, lng_ref, lnb_ref, o_ref):
    f = jnp.dot(a_ref[...], w_ref[...], preferred_element_type=F32)
    o_ref[...] = _res_ln(h_ref[...], f, g_ref[0], lng_ref[...], lnb_ref[...])


def _outproj(a, w, h, mod, gate_chunk, lng, lnb, rows):
    tm = 256
    return pl.pallas_call(
        _outproj_kernel,
        grid=(rows // tm,),
        in_specs=[pl.BlockSpec((tm, D_MODEL), lambda i: (i, 0)),
                  pl.BlockSpec((D_MODEL, D_MODEL), lambda i: (0, 0)),
                  pl.BlockSpec((tm, D_MODEL), lambda i: (i, 0)),
                  _mod_spec(gate_chunk, tm),
                  pl.BlockSpec((1, D_MODEL), lambda i: (0, 0)),
                  pl.BlockSpec((1, D_MODEL), lambda i: (0, 0))],
        out_specs=pl.BlockSpec((tm, D_MODEL), lambda i: (i, 0)),
        out_shape=jax.ShapeDtypeStruct((rows, D_MODEL), F32),
        compiler_params=_cparams("arbitrary"),
        name="out_proj",
    )(a, w, h, mod, lng, lnb)


def _ffn_kernel(h_ref, sh_ref, sc_ref, g_ref, w1_ref, w3_ref, w2_ref, lng_ref, lnb_ref, o_ref, u_ref, acc_ref):
    f = pl.program_id(1)

    @pl.when(f == 0)
    def _():
        u = _layer_norm(h_ref[...]) * (1.0 + sc_ref[0]) + sh_ref[0]
        u_ref[...] = u.astype(BF16)
        acc_ref[...] = jnp.zeros(acc_ref.shape, F32)

    u = u_ref[...]
    h1 = jnp.dot(u, w1_ref[...], preferred_element_type=F32)
    h3 = jnp.dot(u, w3_ref[...], preferred_element_type=F32)
    act = (h1 * _sigmoid(h1) * h3).astype(BF16)
    acc_ref[...] += jnp.dot(act, w2_ref[...], preferred_element_type=F32)

    @pl.when(f == pl.num_programs(1) - 1)
    def _():
        o_ref[...] = _res_ln(h_ref[...], acc_ref[...], g_ref[0], lng_ref[...], lnb_ref[...])


def _ffn(h, mod, w1, w3, w2, lng, lnb):
    rows = h.shape[0]
    tm, tf = 512, 512
    return pl.pallas_call(
        _ffn_kernel,
        grid=(rows // tm, D_FF // tf),
        in_specs=[pl.BlockSpec((tm, D_MODEL), lambda i, f: (i, 0)),
                  _mod_spec(3, tm), _mod_spec(4, tm), _mod_spec(5, tm),
                  pl.BlockSpec((D_MODEL, tf), lambda i, f: (0, f)),
                  pl.BlockSpec((D_MODEL, tf), lambda i, f: (0, f)),
                  pl.BlockSpec((tf, D_MODEL), lambda i, f: (f, 0)),
                  pl.BlockSpec((1, D_MODEL), lambda i, f: (0, 0)),
                  pl.BlockSpec((1, D_MODEL), lambda i, f: (0, 0))],
        out_specs=pl.BlockSpec((tm, D_MODEL), lambda i, f: (i, 0)),
        out_shape=jax.ShapeDtypeStruct((rows, D_MODEL), F32),
        scratch_shapes=[pltpu.VMEM((tm, D_MODEL), BF16), pltpu.VMEM((tm, D_MODEL), F32)],
        compiler_params=_cparams("arbitrary", "arbitrary"),
        name="ffn_dense",
    )(h, mod, mod, mod, w1, w3, w2, lng, lnb)


def _route_kernel(h_ref, sh_ref, sc_ref, rw_ref, rb_ref, t_ref, idx_ref, gate_ref):
    u = _layer_norm(h_ref[...]) * (1.0 + sc_ref[0]) + sh_ref[0]
    u = u.astype(BF16)
    t_ref[...] = u.astype(F32)
    lane = lax.broadcasted_iota(jnp.int32, (h_ref.shape[0], LANES), 1)
    logits = jnp.dot(u, rw_ref[...], preferred_element_type=F32) + rb_ref[...]
    logits = jnp.where(lane < N_EXPERTS, logits, -jnp.inf)
    v1 = jnp.max(logits, axis=-1, keepdims=True)
    i1 = jnp.min(jnp.where(logits == v1, lane, LANES), axis=-1, keepdims=True)
    rest = jnp.where(lane == i1, -jnp.inf, logits)
    v2 = jnp.max(rest, axis=-1, keepdims=True)
    i2 = jnp.min(jnp.where(rest == v2, lane, LANES), axis=-1, keepdims=True)
    e2 = jnp.exp(v2 - v1)
    g1 = 1.0 / (1.0 + e2)
    idx_ref[...] = jnp.where(lane == 0, i1, jnp.where(lane == 1, i2, 0))
    gate_ref[...] = jnp.where(lane == 0, g1, jnp.where(lane == 1, e2 * g1, 0.0))


def _route(h, mod, rw, rb):
    rows = h.shape[0]
    tm = 512
    return pl.pallas_call(
        _route_kernel,
        grid=(rows // tm,),
        in_specs=[pl.BlockSpec((tm, D_MODEL), lambda i: (i, 0)),
                  _mod_spec(3, tm), _mod_spec(4, tm),
                  pl.BlockSpec((D_MODEL, LANES), lambda i: (0, 0)),
                  pl.BlockSpec((1, LANES), lambda i: (0, 0))],
        out_specs=[pl.BlockSpec((tm, D_MODEL), lambda i: (i, 0)),
                   pl.BlockSpec((tm, LANES), lambda i: (i, 0)),
                   pl.BlockSpec((tm, LANES), lambda i: (i, 0))],
        out_shape=[jax.ShapeDtypeStruct((rows, D_MODEL), F32), jax.ShapeDtypeStruct((rows, LANES), jnp.int32),
                   jax.ShapeDtypeStruct((rows, LANES), F32)],
        compiler_params=_cparams("arbitrary"),
        name="moe_route",
    )(h, mod, mod, rw, rb)


def _gather_start(src_hbm, idx_ref, idx0, dst_ref, n, sem):
    def issue(r, carry):
        row = idx_ref[0, 0, idx0 + r]
        pltpu.make_async_copy(src_hbm.at[pl.ds(row, 1)], dst_ref.at[pl.ds(r, 1)], sem).start()
        return carry

    lax.fori_loop(0, n, issue, 0)


def _gather_wait(src_hbm, dst_ref, n, sem):
    pltpu.make_async_copy(src_hbm.at[pl.ds(0, n)], dst_ref, sem).wait()


def _experts_kernel(te_ref, tv_ref, tok_ref, t_hbm, w1_ref, w3_ref, w2_ref, o_ref, xf_ref, x_ref, acc_ref, sem):
    i = pl.program_id(0)
    f = pl.program_id(1)
    live = tv_ref[i] > 0

    @pl.when(jnp.logical_and(live, f == 0))
    def _():
        _gather_start(t_hbm, tok_ref, 0, xf_ref, MOE_TILE, sem)
        _gather_wait(t_hbm, xf_ref, MOE_TILE, sem)
        x_ref[...] = xf_ref[...].astype(BF16)
        acc_ref[...] = jnp.zeros(acc_ref.shape, F32)

    @pl.when(live)
    def _():
        x = x_ref[...]
        h1 = jnp.dot(x, w1_ref[...], preferred_element_type=F32)
        h3 = jnp.dot(x, w3_ref[...], preferred_element_type=F32)
        act = (h1 * _sigmoid(h1) * h3).astype(BF16)
        acc_ref[...] += jnp.dot(act, w2_ref[...], preferred_element_type=F32)

    @pl.when(f == pl.num_programs(1) - 1)
    def _():
        o_ref[...] = jnp.where(live, acc_ref[...], 0.0)


def _experts(tile_expert, tile_live, row_token, t, w1, w3, w2):
    tf = 512
    grid_spec = pltpu.PrefetchScalarGridSpec(
        num_scalar_prefetch=2,
        grid=(MOE_TILES, D_FF // tf),
        in_specs=[pl.BlockSpec((1, 1, MOE_TILE), lambda i, f, te, tv: (i, 0, 0), memory_space=pltpu.SMEM),
                  pl.BlockSpec(memory_space=pl.ANY),
                  pl.BlockSpec((None, D_MODEL, tf), lambda i, f, te, tv: (te[i], 0, f)),
                  pl.BlockSpec((None, D_MODEL, tf), lambda i, f, te, tv: (te[i], 0, f)),
                  pl.BlockSpec((None, tf, D_MODEL), lambda i, f, te, tv: (te[i], f, 0))],
        out_specs=pl.BlockSpec((MOE_TILE, D_MODEL), lambda i, f, te, tv: (i, 0)),
        scratch_shapes=[pltpu.VMEM((MOE_TILE, D_MODEL), F32), pltpu.VMEM((MOE_TILE, D_MODEL), BF16),
                        pltpu.VMEM((MOE_TILE, D_MODEL), F32), pltpu.SemaphoreType.DMA],
    )
    return pl.pallas_call(
        _experts_kernel,
        grid_spec=grid_spec,
        out_shape=jax.ShapeDtypeStruct((MOE_TILES * MOE_TILE, D_MODEL), F32),
        compiler_params=_cparams("arbitrary", "arbitrary"),
        name="moe_experts",
    )(tile_expert, tile_live, row_token, t, w1, w3, w2)


def _combine_kernel(pos_ref, y_hbm, g_ref, h_ref, gm_ref, lng_ref, lnb_ref, o_ref, buf_ref, sem):
    tm = h_ref.shape[0]
    _gather_start(y_hbm, pos_ref, 0, buf_ref.at[0], tm, sem.at[0])
    _gather_start(y_hbm, pos_ref, tm, buf_ref.at[1], tm, sem.at[1])
    _gather_wait(y_hbm, buf_ref.at[0], tm, sem.at[0])
    _gather_wait(y_hbm, buf_ref.at[1], tm, sem.at[1])
    g = g_ref[...]
    f = g[:, 0:1] * buf_ref[0] + g[:, 1:2] * buf_ref[1]
    o_ref[...] = _res_ln(h_ref[...], f, gm_ref[0], lng_ref[...], lnb_ref[...])


def _combine(pos, yb, gates, h, mod, lng, lnb):
    rows = h.shape[0]
    tm = pos.shape[-1] // 2
    return pl.pallas_call(
        _combine_kernel,
        grid=(rows // tm,),
        in_specs=[pl.BlockSpec((1, 1, 2 * tm), lambda i: (i, 0, 0), memory_space=pltpu.SMEM),
                  pl.BlockSpec(memory_space=pl.ANY),
                  pl.BlockSpec((tm, LANES), lambda i: (i, 0)),
                  pl.BlockSpec((tm, D_MODEL), lambda i: (i, 0)),
                  _mod_spec(5, tm),
                  pl.BlockSpec((1, D_MODEL), lambda i: (0, 0)),
                  pl.BlockSpec((1, D_MODEL), lambda i: (0, 0))],
        out_specs=pl.BlockSpec((tm, D_MODEL), lambda i: (i, 0)),
        out_shape=jax.ShapeDtypeStruct((rows, D_MODEL), F32),
        scratch_shapes=[pltpu.VMEM((2, tm, D_MODEL), F32), pltpu.SemaphoreType.DMA((2,))],
        compiler_params=_cparams("arbitrary"),
        name="moe_combine",
    )(pos, yb, gates, h, mod, lng, lnb)


def _cast_kernel(w_ref, o_ref):
    o_ref[...] = w_ref[...].astype(BF16)


def _to_bf16(w, tr):
    r, c = w.shape
    return pl.pallas_call(
        _cast_kernel,
        grid=(r // tr,),
        in_specs=[pl.BlockSpec((tr, c), lambda i: (i, 0))],
        out_specs=pl.BlockSpec((tr, c), lambda i: (i, 0)),
        out_shape=jax.ShapeDtypeStruct((r, c), BF16),
        compiler_params=_cparams("arbitrary"),
        name="cast_bf16",
    )(w)


def _moe(h, mod, router_w, router_b, w1, w3, w2, lng, lnb):
    n = h.shape[0]
    rw = jnp.zeros((D_MODEL, LANES), BF16).at[:, :N_EXPERTS].set(router_w.astype(BF16))
    rb = jnp.zeros((1, LANES), F32).at[0, :N_EXPERTS].set(router_b)
    t, idx, gates = _route(h, mod, rw, rb)
    expert = idx[:, :TOP_K].reshape(-1)
    onehot = (expert[:, None] == jnp.arange(N_EXPERTS)[None, :]).astype(jnp.int32)
    csum = jnp.cumsum(onehot, axis=0)
    rank = jnp.sum(csum * onehot, axis=1) - 1
    counts = csum[-1]
    padded = (counts + MOE_TILE - 1) // MOE_TILE * MOE_TILE
    pend = jnp.cumsum(padded)
    pos = (pend - padded)[expert] + rank
    n_rows = MOE_TILES * MOE_TILE
    row_token = jnp.zeros((n_rows,), jnp.int32).at[pos].set(jnp.arange(n * TOP_K, dtype=jnp.int32) // TOP_K)
    tile_start = jnp.arange(MOE_TILES) * MOE_TILE
    tile_expert = jnp.minimum(jnp.searchsorted(pend, tile_start, side='right'), N_EXPERTS - 1).astype(jnp.int32)
    tile_live = (tile_start < pend[-1]).astype(jnp.int32)
    yb = _experts(tile_expert, tile_live, row_token.reshape(MOE_TILES, 1, MOE_TILE), t, w1, w3, w2)
    tm = 256
    pos2 = pos.reshape(n // tm, tm, TOP_K).transpose(0, 2, 1).reshape(n // tm, 1, TOP_K * tm).astype(jnp.int32)
    return _combine(pos2, yb, gates, h, mod, lng, lnb)


def _to_colmajor(t):
    b, l, c = t.shape
    return t.reshape(b, l // GRID_W, GRID_W, c).transpose(0, 2, 1, 3).reshape(b, l, c)


def _seq_layout(p_cols, pad):
    c = p_cols.shape[-1]
    lat = _to_colmajor(p_cols[:N_LAT].reshape(BATCH, SEQ, c))
    ctx = p_cols[N_LAT:].reshape(BATCH, CTX_LEN, c)
    if not pad:
        return jnp.concatenate([ctx, lat], axis=1)
    z = jnp.zeros((BATCH, CONV_PAD, c), p_cols.dtype)
    return jnp.concatenate([z, ctx, z, lat, z], axis=1)


def _token_layout(s):
    c = s.shape[-1]
    lat = s[:, CTX_LEN:].reshape(BATCH, GRID_W, SEQ // GRID_W, c).transpose(0, 2, 1, 3).reshape(N_LAT, c)
    ctx = s[:, :CTX_LEN].reshape(N_CTX, c)
    return jnp.concatenate([lat, ctx], axis=0)


def _mixer(h, mod, w_in, conv_rg_w, conv_rg_b, conv_m_w, conv_m_b, rg_wa, rg_ba, rg_wx, rg_bx, rg_lam,
           m_gate_b, m_gn_g, p_rg, p_m, rows_out):
    gate0 = C_M + 4 * W_M
    w_gate = jnp.zeros((D_MODEL, LANES), BF16).at[:, :N_GATES].set(w_in[:, gate0:gate0 + N_GATES].astype(BF16))
    u, pg = _lnmod(h, mod, w_gate)
    p = _proj(u, w_in, gate0)
    p2 = _proj(u, w_in[:, gate0 + N_GATES:], 2 * D_MODEL)

    wcat = jnp.concatenate([rg_wa[0], rg_wx[0], rg_wa[1], rg_wx[1]], axis=-1).astype(BF16)
    bcat = jnp.stack([rg_ba[0], rg_bx[0], rg_ba[1], rg_bx[1]], axis=0)
    bcat = bcat.reshape(4, RG_BLOCKS, RG_BLOCK).transpose(1, 0, 2).reshape(RG_BLOCKS, 1, 4 * RG_BLOCK)
    cb = conv_rg_b.reshape(1, W_RG)
    h0 = jnp.zeros((BATCH, 2, W_RG), F32)
    yr_c, st = _rg_branch(p, h0, conv_rg_w, cb, wcat, bcat, rg_lam, seq=CTX_LEN, row_block0=N_LAT // CTX_LEN)
    yr, _ = _rg_branch(p, st, conv_rg_w, cb, wcat, bcat, rg_lam, seq=SEQ, row_block0=0)
    if rows_out > N_LAT:
        yr = jnp.concatenate([yr, yr_c], axis=0)

    qk_in = _seq_layout(p[:, COL_Q * D_MODEL:(COL_K + 1) * D_MODEL], True)
    v_seq = _seq_layout(p[:, COL_V * D_MODEL:(COL_V + 1) * D_MODEL].astype(BF16), False)
    g_seq = _seq_layout(pg, False)
    qscale = jnp.concatenate([jnp.full((1, W_M), M_HEAD_DIM ** -0.5, F32), jnp.ones((1, W_M), F32)], axis=1)
    qk = _qkconv(qk_in, conv_m_w, conv_m_b.reshape(1, 2 * W_M), qscale)
    gate_b = jnp.zeros((1, LANES), F32).at[0, :N_GATES].set(m_gate_b.reshape(-1))
    h_f, h_b = _mlstm(qk, v_seq, g_seq, gate_b)
    hs = _token_layout(h_f + h_b)

    return _merge(yr, hs, p, p2, m_gn_g.reshape(1, W_M), p_rg.astype(BF16), p_m.astype(BF16), rows_out)


def kernel(x, c, ctx, c_ctx, w_mod, b_mod, w_in, conv_rg_w, conv_rg_b, conv_m_w, conv_m_b, rg_wa, rg_ba, rg_wx, rg_bx, rg_lam, m_gate_b, m_gn_g, p_rg, p_m, w_out, ln_g, ln_b, ff_w1, ff_w3, ff_w2, router_w, router_b, ex_w1, ex_w3, ex_w2):
    h = jnp.concatenate([x.reshape(N_LAT, D_MODEL), ctx.reshape(N_CTX, D_MODEL)], axis=0)
    cond = jnp.zeros((SUBLANES, D_MODEL), F32).at[:BATCH].set(c).at[BATCH].set(c_ctx)
    for l in range(DEPTH):
        last = l == DEPTH - 1
        rows = N_LAT if last else N_TOK
        mod = _modvec(cond, w_mod[l], b_mod[l])
        mix = _mixer(h, mod, w_in[l], conv_rg_w[l], conv_rg_b[l], conv_m_w[l], conv_m_b[l], rg_wa[l], rg_ba[l],
                     rg_wx[l], rg_bx[l], rg_lam[l], m_gate_b[l], m_gn_g[l], p_rg[l], p_m[l], rows)
        lng = ln_g[l].reshape(2, 1, D_MODEL)
        lnb = ln_b[l].reshape(2, 1, D_MODEL)
        h = _outproj(mix, w_out[l].astype(BF16), h, mod, 2, lng[0], lnb[0], rows)
        k = l // 2
        if l % 2 == 0:
            h = _ffn(h, mod, _to_bf16(ff_w1[k], 256), _to_bf16(ff_w3[k], 256), _to_bf16(ff_w2[k], 1024),
                     lng[1], lnb[1])
        else:
            w1 = _to_bf16(ex_w1[k].reshape(N_EXPERTS * D_MODEL, D_FF), 256).reshape(N_EXPERTS, D_MODEL, D_FF)
            w3 = _to_bf16(ex_w3[k].reshape(N_EXPERTS * D_MODEL, D_FF), 256).reshape(N_EXPERTS, D_MODEL, D_FF)
            w2 = _to_bf16(ex_w2[k].reshape(N_EXPERTS * D_FF, D_MODEL), 1024).reshape(N_EXPERTS, D_FF, D_MODEL)
            h = _moe(h, mod, router_w[k], router_b[k], w1, w3, w2, lng[1], lnb[1])
    return h[:N_LAT].reshape(BATCH, SEQ, D_MODEL)
```

```python
import functools

import jax
import jax.numpy as jnp
from jax import lax
from jax.experimental import pallas as pl
from jax.experimental.pallas import tpu as pltpu

F32 = jnp.float32
BF16 = jnp.bfloat16

D_MODEL = 2048
BATCH = 4
SEQ = 4096
DEPTH = 2
GRID_W = 64
CTX_LEN = 256
W_RG = 2048
RG_BLOCKS = 16
RG_BLOCK = W_RG // RG_BLOCKS
RG_C = 8.0
CONV_W = 4
M_HEADS = 8
M_HEAD_DIM = 256
W_M = M_HEADS * M_HEAD_DIM
M_CHUNK = 128
C_M = 2 * W_RG
N_GATES = 4 * M_HEADS
D_FF = 7168
N_EXPERTS = 8
TOP_K = 2
ALPHA = (2.0 * DEPTH) ** 0.25
LN_EPS = 1e-6

N_LAT = BATCH * SEQ
N_CTX = BATCH * CTX_LEN
N_TOK = N_LAT + N_CTX
LANES = 128
SUBLANES = 8
VMEM_LIMIT = 52 * 1024 * 1024

COL_XR, COL_RGG, COL_Q, COL_K, COL_V, COL_O = range(6)

SEQ_ALL = CTX_LEN + SEQ
CONV_PAD = 8
SEQ_PADDED = CONV_PAD + CTX_LEN + CONV_PAD + SEQ + CONV_PAD
MOE_TILE = 512
MOE_TILES = (N_LAT * TOP_K) // MOE_TILE + N_EXPERTS


def _cparams(*sem):
    return pltpu.CompilerParams(dimension_semantics=sem, vmem_limit_bytes=VMEM_LIMIT)


def _layer_norm(x):
    mu = jnp.mean(x, axis=-1, keepdims=True)
    xc = x - mu
    var = jnp.mean(xc * xc, axis=-1, keepdims=True)
    return xc * lax.rsqrt(var + LN_EPS)


def _sigmoid(x):
    return 0.5 * jnp.tanh(0.5 * x) + 0.5


def _softplus(x):
    return jnp.maximum(x, 0.0) + jnp.log1p(jnp.exp(-jnp.abs(x)))


def _mod_row(i, tm):
    return jnp.where(i >= N_LAT // tm, BATCH, i // (SEQ // tm))


def _mod_spec(chunk, tm):
    return pl.BlockSpec((1, 1, D_MODEL), lambda i, *_: (_mod_row(i, tm), 0, chunk))


def _modvec_kernel(s_ref, w_ref, b_ref, o_ref):
    s = s_ref[...]
    s = s * _sigmoid(s)
    o_ref[...] = jnp.dot(s.astype(BF16), w_ref[...].astype(BF16), preferred_element_type=F32) + b_ref[...]


def _modvec(cond, w, b, layer):
    n = w.shape[-1]
    tn = 1024
    out = pl.pallas_call(
        _modvec_kernel,
        grid=(n // tn,),
        in_specs=[pl.BlockSpec((SUBLANES, D_MODEL), lambda j: (0, 0)),
                  pl.BlockSpec((None, D_MODEL, tn), lambda j: (layer, 0, j)),
                  pl.BlockSpec((1, tn), lambda j: (0, j))],
        out_specs=pl.BlockSpec((SUBLANES, tn), lambda j: (0, j)),
        out_shape=jax.ShapeDtypeStruct((SUBLANES, n), F32),
        compiler_params=_cparams("arbitrary"),
        name="modvec",
    )(cond, w, b.reshape(1, n))
    return out.reshape(SUBLANES, 1, n)


def _lnmod_kernel(x_ref, sh_ref, sc_ref, wg_ref, u_ref, g_ref):
    u = (_layer_norm(x_ref[...]) * (1.0 + sc_ref[0]) + sh_ref[0]).astype(BF16)
    u_ref[...] = u
    g_ref[...] = jnp.dot(u, wg_ref[...], preferred_element_type=F32)


def _lnmod(h, mod, wg):
    r = h.shape[0]
    tm = 512
    return pl.pallas_call(
        _lnmod_kernel,
        grid=(r // tm,),
        in_specs=[pl.BlockSpec((tm, D_MODEL), lambda i: (i, 0)),
                  _mod_spec(0, tm), _mod_spec(1, tm),
                  pl.BlockSpec((D_MODEL, LANES), lambda i: (0, 0))],
        out_specs=[pl.BlockSpec((tm, D_MODEL), lambda i: (i, 0)),
                   pl.BlockSpec((tm, LANES), lambda i: (i, 0))],
        out_shape=[jax.ShapeDtypeStruct((r, D_MODEL), BF16), jax.ShapeDtypeStruct((r, LANES), F32)],
        compiler_params=_cparams("arbitrary"),
        name="ln_mod",
    )(h, mod, mod, wg)


def _proj_kernel(u_ref, w_ref, o_ref, wb_ref):
    @pl.when(pl.program_id(1) == 0)
    def _():
        wb_ref[...] = w_ref[...].astype(BF16)

    o_ref[...] = jnp.dot(u_ref[...], wb_ref[...], preferred_element_type=F32)


def _proj(u, w, layer, n):
    r = u.shape[0]
    tm, tn = 1024, 1024
    return pl.pallas_call(
        _proj_kernel,
        grid=(n // tn, r // tm),
        in_specs=[pl.BlockSpec((tm, D_MODEL), lambda j, i: (i, 0)),
                  pl.BlockSpec((None, D_MODEL, tn), lambda j, i: (layer, 0, j))],
        out_specs=pl.BlockSpec((tm, tn), lambda j, i: (i, j)),
        out_shape=jax.ShapeDtypeStruct((r, n), F32),
        scratch_shapes=[pltpu.VMEM((D_MODEL, tn), BF16)],
        compiler_params=_cparams("arbitrary", "arbitrary"),
        name="proj_in",
    )(u, w)


def _rg_kernel(x_ref, g_ref, cw_ref, cb_ref, w_ref, bias_ref, lam_ref, h0_ref, y_ref, hl_ref,
               xs, a0, b0, a1, b1, *, seq, tc, seg, pitch):
    zeros8 = jnp.zeros((CONV_PAD, LANES), F32)
    xs[0:CONV_PAD, :] = zeros8
    xs[CONV_PAD + seq:CONV_PAD + seq + CONV_PAD, :] = zeros8
    xs[CONV_PAD:CONV_PAD + seq, :] = x_ref[...]

    cw = cw_ref[...]
    sp = _softplus(-lam_ref[...])
    dec = ((a0, b0), (a1, b1))
    for c in range(seq // tc):
        r0 = c * tc
        xc = cb_ref[...] + cw[0:1, :] * xs[r0 + CONV_PAD - 2:r0 + CONV_PAD - 2 + tc, :]
        for j in range(1, CONV_W):
            xc = xc + cw[j:j + 1, :] * xs[r0 + CONV_PAD - 2 + j:r0 + CONV_PAD - 2 + j + tc, :]
        z = jnp.dot(xc.astype(BF16), w_ref[...], preferred_element_type=F32) + bias_ref[...]
        for d in range(2):
            rg = _sigmoid(z[:, (2 * d) * LANES:(2 * d + 1) * LANES])
            ig = _sigmoid(z[:, (2 * d + 1) * LANES:(2 * d + 2) * LANES])
            log_a = (-RG_C) * rg * sp[d:d + 1, :]
            a = jnp.exp(log_a)
            b = jnp.sqrt(jnp.tanh(-log_a) * (1.0 + a * a)) * (ig * xc)
            a_ref, b_ref = dec[d]
            for p in range(max(tc // seg, 1)):
                n = min(tc, seg)
                t = r0 + p * n
                dst = (t // seg) * pitch + t % seg
                a_ref[dst:dst + n, :] = a[p * n:(p + 1) * n, :]
                b_ref[dst:dst + n, :] = b[p * n:(p + 1) * n, :]

    def scan_step(i, carry):
        hf, pf, hb, pb = carry
        tf = i
        tb = seg - 1 - i
        af = a0[pl.ds(tf, SUBLANES, stride=pitch), :]
        bf = b0[pl.ds(tf, SUBLANES, stride=pitch), :]
        ab = a1[pl.ds(tb, SUBLANES, stride=pitch), :]
        bb = b1[pl.ds(tb, SUBLANES, stride=pitch), :]
        hf = af * hf + bf
        pf = pf * af
        hb = ab * hb + bb
        pb = pb * ab
        b0[pl.ds(tf, SUBLANES, stride=pitch), :] = hf
        a0[pl.ds(tf, SUBLANES, stride=pitch), :] = pf
        b1[pl.ds(tb, SUBLANES, stride=pitch), :] = hb
        a1[pl.ds(tb, SUBLANES, stride=pitch), :] = pb
        return hf, pf, hb, pb

    z8 = jnp.zeros((SUBLANES, LANES), F32)
    o8 = jnp.ones((SUBLANES, LANES), F32)
    hf, pf, hb, pb = lax.fori_loop(0, seg, scan_step, (z8, o8, z8, o8), unroll=4)

    h0 = h0_ref[...]
    cf = [None] * SUBLANES
    cbk = [None] * SUBLANES
    carry = h0[0:1, :]
    for s in range(SUBLANES):
        cf[s] = carry
        carry = pf[s:s + 1, :] * carry + hf[s:s + 1, :]
    hl_f = carry
    carry = h0[1:2, :]
    for s in reversed(range(SUBLANES)):
        cbk[s] = carry
        carry = pb[s:s + 1, :] * carry + hb[s:s + 1, :]
    hl_b = carry
    hl_ref[...] = jnp.concatenate([hl_f, hl_b], axis=0)

    for s in range(SUBLANES):
        for c in range(max(seg // tc, 1)):
            n = min(tc, seg)
            src = s * pitch + c * n
            t = s * seg + c * n
            h = (b0[src:src + n, :] + a0[src:src + n, :] * cf[s]
                 + b1[src:src + n, :] + a1[src:src + n, :] * cbk[s])
            y_ref[t:t + n, :] = (h * jax.nn.gelu(g_ref[t:t + n, :])).astype(y_ref.dtype)


def _rg_branch(p, h0, cw, cb, wcat, bcat, lam, *, seq, row_block0):
    tc = min(256, seq)
    seg = seq // SUBLANES
    pitch = seg + SUBLANES if (seg // SUBLANES) % 2 == 0 else seg
    nblk = W_RG // LANES
    rows = SUBLANES * pitch
    kern = functools.partial(_rg_kernel, seq=seq, tc=tc, seg=seg, pitch=pitch)
    in_specs = [
        pl.BlockSpec((seq, LANES), lambda b, c: (row_block0 + b, COL_XR * nblk + c)),
        pl.BlockSpec((seq, LANES), lambda b, c: (row_block0 + b, COL_RGG * nblk + c)),
        pl.BlockSpec((CONV_W, LANES), lambda b, c: (0, c)),
        pl.BlockSpec((1, LANES), lambda b, c: (0, c)),
        pl.BlockSpec((None, LANES, 4 * LANES), lambda b, c: (c, 0, 0)),
        pl.BlockSpec((None, 1, 4 * LANES), lambda b, c: (c, 0, 0)),
        pl.BlockSpec((2, LANES), lambda b, c: (0, c)),
        pl.BlockSpec((None, 2, LANES), lambda b, c: (b, 0, c)),
    ]
    out_shape = [jax.ShapeDtypeStruct((BATCH * seq, W_RG), BF16), jax.ShapeDtypeStruct((BATCH, 2, W_RG), F32)]
    out_specs = [pl.BlockSpec((seq, LANES), lambda b, c: (b, c)),
                 pl.BlockSpec((None, 2, LANES), lambda b, c: (b, 0, c))]
    return pl.pallas_call(
        kern,
        grid=(BATCH, nblk),
        in_specs=in_specs,
        out_specs=out_specs,
        out_shape=out_shape,
        scratch_shapes=[pltpu.VMEM((seq + 2 * CONV_PAD, LANES), F32)] + [pltpu.VMEM((rows, LANES), F32)] * 4,
        compiler_params=_cparams("arbitrary", "arbitrary"),
        name=f"rglru_{seq}",
    )(p, p, cw, cb, wcat, bcat, lam, h0)


def _qkconv_kernel(x_ref, cw_ref, cb_ref, sc_ref, o_ref, *, tc):
    cw = cw_ref[...]
    for in_off, out_off, n in ((CONV_PAD, 0, CTX_LEN), (2 * CONV_PAD + CTX_LEN, CTX_LEN, SEQ)):
        for c in range(n // tc):
            r0 = in_off + c * tc - 2
            xc = cb_ref[...] + cw[0:1, :] * x_ref[r0:r0 + tc, :]
            for j in range(1, CONV_W):
                xc = xc + cw[j:j + 1, :] * x_ref[r0 + j:r0 + j + tc, :]
            y = xc * _sigmoid(xc) * sc_ref[...]
            o_ref[out_off + c * tc:out_off + (c + 1) * tc, :] = y.astype(o_ref.dtype)


def _qkconv(x, cw, cb, scale):
    cwid = 256
    n = x.shape[-1]
    return pl.pallas_call(
        functools.partial(_qkconv_kernel, tc=256),
        grid=(BATCH, n // cwid),
        in_specs=[pl.BlockSpec((None, SEQ_PADDED, cwid), lambda b, c: (b, 0, c)),
                  pl.BlockSpec((CONV_W, cwid), lambda b, c: (0, c)),
                  pl.BlockSpec((1, cwid), lambda b, c: (0, c)),
                  pl.BlockSpec((1, cwid), lambda b, c: (0, c))],
        out_specs=pl.BlockSpec((None, SEQ_ALL, cwid), lambda b, c: (b, 0, c)),
        out_shape=jax.ShapeDtypeStruct((BATCH, SEQ_ALL, n), BF16),
        compiler_params=_cparams("arbitrary", "arbitrary"),
        name="qk_conv",
    )(x, cw, cb, scale)


def _lane_cumsum(x, reverse):
    lane = lax.broadcasted_iota(jnp.int32, x.shape, 1)
    s = 1
    while s < LANES:
        if reverse:
            x = x + jnp.where(lane < LANES - s, pltpu.roll(x, LANES - s, axis=1), 0.0)
        else:
            x = x + jnp.where(lane >= s, pltpu.roll(x, s, axis=1), 0.0)
        s *= 2
    return x


def _mlstm_direction(q_ref, k_ref, v_ref, g, o_ref, c_ref, n_ref, m_ref, d):
    reverse = d == 1
    nt = (((1,), (1,)), ((), ()))
    gt = g.T
    base = d * 2 * M_HEADS
    i_rows = gt[base:base + M_HEADS, :]
    f_rows = -_softplus(-gt[base + M_HEADS:base + 2 * M_HEADS, :])
    bcum = _lane_cumsum(f_rows, reverse)
    r_rows = i_rows - bcum
    pad = jnp.zeros((M_CHUNK - M_HEADS, M_CHUNK), F32)
    r_cols = jnp.concatenate([r_rows, pad], axis=0).T
    last = 0 if reverse else M_CHUNK - 1

    src_i = lax.broadcasted_iota(jnp.int32, (M_CHUNK, M_CHUNK), 0)
    dst_i = lax.broadcasted_iota(jnp.int32, (M_CHUNK, M_CHUNK), 1)
    mask = (src_i >= dst_i) if reverse else (src_i <= dst_i)

    for h in range(M_HEADS):
        sl = slice(h * M_HEAD_DIM, (h + 1) * M_HEAD_DIM)
        qh = q_ref[:, sl]
        kh = k_ref[:, sl]
        vt = v_ref[sl, :]
        b_row = bcum[h:h + 1, :]
        m_prev = m_ref[d, h:h + 1, :]
        b_last = jnp.broadcast_to(b_row[:, last:last + 1], (1, M_CHUNK))

        log_d = jnp.where(mask, b_row + r_cols[:, h:h + 1], -jnp.inf)
        log_inter = b_row + m_prev
        m_t = jnp.maximum(jnp.max(log_d, axis=0, keepdims=True), log_inter)
        st = lax.dot_general(kh, qh, nt, preferred_element_type=F32)
        st = st * jnp.exp(log_d - m_t)
        w_inter = jnp.exp(log_inter - m_t)
        c_prev = c_ref[d, h]
        n_prev = n_ref[d, h]
        cq = lax.dot_general(c_prev.astype(BF16), qh, nt, preferred_element_type=F32)
        num = jnp.dot(vt.astype(BF16), st.astype(BF16), preferred_element_type=F32) + w_inter * cq
        qn = lax.dot_general(n_prev.astype(BF16), qh, nt, preferred_element_type=F32)[0:1, :]
        den = jnp.sum(st, axis=0, keepdims=True) + w_inter * qn
        o_ref[sl, :] = num / jnp.maximum(jnp.abs(den), jnp.exp(-m_t))

        log_w = b_last - b_row + i_rows[h:h + 1, :]
        m_new = jnp.maximum(b_last + m_prev, jnp.max(log_w, axis=1, keepdims=True))
        w = jnp.exp(log_w - m_new)
        decay = jnp.exp(b_last + m_prev - m_new)
        vw = (vt * w).astype(BF16)
        c_ref[d, h] = decay[:, 0:1] * c_prev + jnp.dot(vw, kh, preferred_element_type=F32)
        w8 = jnp.broadcast_to(w, (SUBLANES, M_CHUNK)).astype(BF16)
        n_ref[d, h] = decay[:, 0:1] * n_prev + jnp.dot(w8, kh, preferred_element_type=F32)
        m_ref[d, h:h + 1, :] = m_new


def _mlstm_kernel(qf_ref, kf_ref, vf_ref, gf_ref, qb_ref, kb_ref, vb_ref, gb_ref, bias_ref, of_ref, ob_ref,
                  c_ref, n_ref, m_ref):
    @pl.when(pl.program_id(1) == 0)
    def _():
        c_ref[...] = jnp.zeros(c_ref.shape, F32)
        n_ref[...] = jnp.zeros(n_ref.shape, F32)
        m_ref[...] = jnp.zeros(m_ref.shape, F32)

    bias = bias_ref[...]
    _mlstm_direction(qf_ref, kf_ref, vf_ref, gf_ref[...] + bias, of_ref, c_ref, n_ref, m_ref, 0)
    _mlstm_direction(qb_ref, kb_ref, vb_ref, gb_ref[...] + bias, ob_ref, c_ref, n_ref, m_ref, 1)


def _chunk_order(i, reverse):
    if not reverse:
        return i
    nctx = CTX_LEN // M_CHUNK
    ntot = SEQ_ALL // M_CHUNK
    return jnp.where(i < nctx, nctx - 1 - i, ntot + nctx - 1 - i)


def _mlstm(q_and_k, v, g, gate_b):
    def blk(col, reverse, width=W_M):
        return pl.BlockSpec((None, M_CHUNK, width), lambda b, i: (b, _chunk_order(i, reverse), col))

    def blk_t(reverse):
        return pl.BlockSpec((None, W_M, M_CHUNK), lambda b, i: (b, 0, _chunk_order(i, reverse)))

    in_specs = [blk(0, False), blk(1, False), blk_t(False), blk(0, False, LANES),
                blk(0, True), blk(1, True), blk_t(True), blk(0, True, LANES),
                pl.BlockSpec((1, LANES), lambda b, i: (0, 0))]
    out = jax.ShapeDtypeStruct((BATCH, W_M, SEQ_ALL), F32)
    return pl.pallas_call(
        _mlstm_kernel,
        grid=(BATCH, SEQ_ALL // M_CHUNK),
        in_specs=in_specs,
        out_specs=[blk_t(False), blk_t(True)],
        out_shape=[out, out],
        scratch_shapes=[pltpu.VMEM((2, M_HEADS, M_HEAD_DIM, M_HEAD_DIM), F32),
                        pltpu.VMEM((2, M_HEADS, SUBLANES, M_HEAD_DIM), F32),
                        pltpu.VMEM((2, M_HEADS, LANES), F32)],
        compiler_params=_cparams("arbitrary", "arbitrary"),
        name="mlstm",
    )(q_and_k, q_and_k, v, g, q_and_k, q_and_k, v, g, gate_b)


def _merge_kernel(yr_ref, hs_ref, o_ref, ga_ref, gb_ref, gn_ref, prg_ref, pm_ref, mix_ref):
    hs = hs_ref[...]
    hn = jnp.concatenate([_layer_norm(hs[:, h * M_HEAD_DIM:(h + 1) * M_HEAD_DIM]) for h in range(M_HEADS)], axis=1)
    ym = (hn * gn_ref[...] * _sigmoid(o_ref[...])).astype(BF16)
    a = jnp.dot(yr_ref[...], prg_ref[...], preferred_element_type=F32)
    b = jnp.dot(ym, pm_ref[...], preferred_element_type=F32)
    mix_ref[...] = (_sigmoid(ga_ref[...]) * a + _sigmoid(gb_ref[...]) * b).astype(BF16)


def _merge(yr, hs, p1, p2, gn, p_rg, p_m, rows):
    tm, tn = 256, 1024
    nj = D_MODEL // tn
    return pl.pallas_call(
        _merge_kernel,
        grid=(nj, rows // tm),
        in_specs=[pl.BlockSpec((tm, W_RG), lambda j, i: (i, 0)),
                  pl.BlockSpec((tm, W_M), lambda j, i: (i, 0)),
                  pl.BlockSpec((tm, W_M), lambda j, i: (i, COL_O)),
                  pl.BlockSpec((tm, tn), lambda j, i: (i, j)),
                  pl.BlockSpec((tm, tn), lambda j, i: (i, nj + j)),
                  pl.BlockSpec((1, W_M), lambda j, i: (0, 0)),
                  pl.BlockSpec((W_RG, tn), lambda j, i: (0, j)),
                  pl.BlockSpec((W_M, tn), lambda j, i: (0, j))],
        out_specs=pl.BlockSpec((tm, tn), lambda j, i: (i, j)),
        out_shape=jax.ShapeDtypeStruct((rows, D_MODEL), BF16),
        compiler_params=_cparams("arbitrary", "arbitrary"),
        name="merge",
    )(yr, hs, p1, p2, p2, gn, p_rg, p_m)


def _res_ln(h, f, gate, lng, lnb):
    y = _layer_norm(ALPHA * h + gate * f)
    return y * lng + lnb


def _outproj_kernel(a_ref, w_ref, h_ref, g_ref, lng_ref, lnb_ref, o_ref):
    f = jnp.dot(a_ref[...], w_ref[...], preferred_element_type=F32)
    o_ref[...] = _res_ln(h_ref[...], f, g_ref[0], lng_ref[...], lnb_ref[...])


def _outproj(a, w, h, mod, gate_chunk, lng, lnb, rows):
    tm = 256
    return pl.pallas_call(
        _outproj_kernel,
        grid=(rows // tm,),
        in_specs=[pl.BlockSpec((tm, D_MODEL), lambda i: (i, 0)),
                  pl.BlockSpec((D_MODEL, D_MODEL), lambda i: (0, 0)),
                  pl.BlockSpec((tm, D_MODEL), lambda i: (i, 0)),
                  _mod_spec(gate_chunk, tm),
                  pl.BlockSpec((1, D_MODEL), lambda i: (0, 0)),
                  pl.BlockSpec((1, D_MODEL), lambda i: (0, 0))],
        out_specs=pl.BlockSpec((tm, D_MODEL), lambda i: (i, 0)),
        out_shape=jax.ShapeDtypeStruct((rows, D_MODEL), F32),
        compiler_params=_cparams("arbitrary"),
        name="out_proj",
    )(a, w, h, mod, lng, lnb)


def _ffn_kernel(h_ref, sh_ref, sc_ref, g_ref, w1_ref, w3_ref, w2_ref, lng_ref, lnb_ref, o_ref, u_ref, acc_ref):
    f = pl.program_id(1)

    @pl.when(f == 0)
    def _():
        u = _layer_norm(h_ref[...]) * (1.0 + sc_ref[0]) + sh_ref[0]
        u_ref[...] = u.astype(BF16)
        acc_ref[...] = jnp.zeros(acc_ref.shape, F32)

    u = u_ref[...]
    h1 = jnp.dot(u, w1_ref[...], preferred_element_type=F32)
    h3 = jnp.dot(u, w3_ref[...], preferred_element_type=F32)
    act = (h1 * _sigmoid(h1) * h3).astype(BF16)
    acc_ref[...] += jnp.dot(act, w2_ref[...], preferred_element_type=F32)

    @pl.when(f == pl.num_programs(1) - 1)
    def _():
        o_ref[...] = _res_ln(h_ref[...], acc_ref[...], g_ref[0], lng_ref[...], lnb_ref[...])


def _ffn(h, mod, w1, w3, w2, lng, lnb):
    rows = h.shape[0]
    tm, tf = 512, 512
    return pl.pallas_call(
        _ffn_kernel,
        grid=(rows // tm, D_FF // tf),
        in_specs=[pl.BlockSpec((tm, D_MODEL), lambda i, f: (i, 0)),
                  _mod_spec(3, tm), _mod_spec(4, tm), _mod_spec(5, tm),
                  pl.BlockSpec((D_MODEL, tf), lambda i, f: (0, f)),
                  pl.BlockSpec((D_MODEL, tf), lambda i, f: (0, f)),
                  pl.BlockSpec((tf, D_MODEL), lambda i, f: (f, 0)),
                  pl.BlockSpec((1, D_MODEL), lambda i, f: (0, 0)),
                  pl.BlockSpec((1, D_MODEL), lambda i, f: (0, 0))],
        out_specs=pl.BlockSpec((tm, D_MODEL), lambda i, f: (i, 0)),
        out_shape=jax.ShapeDtypeStruct((rows, D_MODEL), F32),
        scratch_shapes=[pltpu.VMEM((tm, D_MODEL), BF16), pltpu.VMEM((tm, D_MODEL), F32)],
        compiler_params=_cparams("arbitrary", "arbitrary"),
        name="ffn_dense",
    )(h, mod, mod, mod, w1, w3, w2, lng, lnb)


def _route_kernel(h_ref, sh_ref, sc_ref, rw_ref, rb_ref, t_ref, idx_ref, gate_ref):
    u = _layer_norm(h_ref[...]) * (1.0 + sc_ref[0]) + sh_ref[0]
    u = u.astype(BF16)
    t_ref[...] = u.astype(F32)
    lane = lax.broadcasted_iota(jnp.int32, (h_ref.shape[0], LANES), 1)
    logits = jnp.dot(u, rw_ref[...], preferred_element_type=F32) + rb_ref[...]
    logits = jnp.where(lane < N_EXPERTS, logits, -jnp.inf)
    v1 = jnp.max(logits, axis=-1, keepdims=True)
    i1 = jnp.min(jnp.where(logits == v1, lane, LANES), axis=-1, keepdims=True)
    rest = jnp.where(lane == i1, -jnp.inf, logits)
    v2 = jnp.max(rest, axis=-1, keepdims=True)
    i2 = jnp.min(jnp.where(rest == v2, lane, LANES), axis=-1, keepdims=True)
    e2 = jnp.exp(v2 - v1)
    g1 = 1.0 / (1.0 + e2)
    idx_ref[...] = jnp.where(lane == 0, i1, jnp.where(lane == 1, i2, 0))
    gate_ref[...] = jnp.where(lane == 0, g1, jnp.where(lane == 1, e2 * g1, 0.0))


def _route(h, mod, rw, rb):
    rows = h.shape[0]
    tm = 512
    return pl.pallas_call(
        _route_kernel,
        grid=(rows // tm,),
        in_specs=[pl.BlockSpec((tm, D_MODEL), lambda i: (i, 0)),
                  _mod_spec(3, tm), _mod_spec(4, tm),
                  pl.BlockSpec((D_MODEL, LANES), lambda i: (0, 0)),
                  pl.BlockSpec((1, LANES), lambda i: (0, 0))],
        out_specs=[pl.BlockSpec((tm, D_MODEL), lambda i: (i, 0)),
                   pl.BlockSpec((tm, LANES), lambda i: (i, 0)),
                   pl.BlockSpec((tm, LANES), lambda i: (i, 0))],
        out_shape=[jax.ShapeDtypeStruct((rows, D_MODEL), F32), jax.ShapeDtypeStruct((rows, LANES), jnp.int32),
                   jax.ShapeDtypeStruct((rows, LANES), F32)],
        compiler_params=_cparams("arbitrary"),
        name="moe_route",
    )(h, mod, mod, rw, rb)


def _gather_start(src_hbm, idx_ref, idx0, dst_ref, n, sem):
    def issue(r, carry):
        row = idx_ref[0, 0, idx0 + r]
        pltpu.make_async_copy(src_hbm.at[pl.ds(row, 1)], dst_ref.at[pl.ds(r, 1)], sem).start()
        return carry

    lax.fori_loop(0, n, issue, 0)


def _gather_wait(src_hbm, dst_ref, n, sem):
    pltpu.make_async_copy(src_hbm.at[pl.ds(0, n)], dst_ref, sem).wait()


def _experts_kernel(te_ref, tv_ref, tok_ref, nxt_ref, t_hbm, w1_ref, w3_ref, w2_ref, o_ref,
                    xf_ref, x_ref, acc_ref, sem):
    i = pl.program_id(0)
    f = pl.program_id(1)
    last_tile = pl.num_programs(0) - 1
    live = tv_ref[i] > 0
    next_live = jnp.logical_and(i < last_tile, tv_ref[jnp.minimum(i + 1, last_tile)] > 0)
    slot = i % 2

    @pl.when(jnp.logical_and(live, jnp.logical_and(f == 0, i == 0)))
    def _():
        _gather_start(t_hbm, tok_ref, 0, xf_ref.at[0], MOE_TILE, sem.at[0])

    @pl.when(jnp.logical_and(live, f == 0))
    def _():
        _gather_wait(t_hbm, xf_ref.at[slot], MOE_TILE, sem.at[slot])
        x_ref[...] = xf_ref[slot].astype(BF16)
        acc_ref[...] = jnp.zeros(acc_ref.shape, F32)

    @pl.when(jnp.logical_and(next_live, f == 0))
    def _():
        _gather_start(t_hbm, nxt_ref, 0, xf_ref.at[1 - slot], MOE_TILE, sem.at[1 - slot])

    @pl.when(live)
    def _():
        x = x_ref[...]
        h1 = jnp.dot(x, w1_ref[...], preferred_element_type=F32)
        h3 = jnp.dot(x, w3_ref[...], preferred_element_type=F32)
        act = (h1 * _sigmoid(h1) * h3).astype(BF16)
        acc_ref[...] += jnp.dot(act, w2_ref[...], preferred_element_type=F32)

    @pl.when(f == pl.num_programs(1) - 1)
    def _():
        o_ref[...] = jnp.where(live, acc_ref[...], 0.0)


def _experts(tile_expert, tile_live, row_token, t, w1, w3, w2):
    tf = 512
    grid_spec = pltpu.PrefetchScalarGridSpec(
        num_scalar_prefetch=2,
        grid=(MOE_TILES, D_FF // tf),
        in_specs=[pl.BlockSpec((1, 1, MOE_TILE), lambda i, f, te, tv: (i, 0, 0), memory_space=pltpu.SMEM),
                  pl.BlockSpec((1, 1, MOE_TILE), lambda i, f, te, tv: (jnp.minimum(i + 1, MOE_TILES - 1), 0, 0),
                               memory_space=pltpu.SMEM),
                  pl.BlockSpec(memory_space=pl.ANY),
                  pl.BlockSpec((None, D_MODEL, tf), lambda i, f, te, tv: (te[i], 0, f)),
                  pl.BlockSpec((None, D_MODEL, tf), lambda i, f, te, tv: (te[i], 0, f)),
                  pl.BlockSpec((None, tf, D_MODEL), lambda i, f, te, tv: (te[i], f, 0))],
        out_specs=pl.BlockSpec((MOE_TILE, D_MODEL), lambda i, f, te, tv: (i, 0)),
        scratch_shapes=[pltpu.VMEM((2, MOE_TILE, D_MODEL), F32), pltpu.VMEM((MOE_TILE, D_MODEL), BF16),
                        pltpu.VMEM((MOE_TILE, D_MODEL), F32), pltpu.SemaphoreType.DMA((2,))],
    )
    return pl.pallas_call(
        _experts_kernel,
        grid_spec=grid_spec,
        out_shape=jax.ShapeDtypeStruct((MOE_TILES * MOE_TILE, D_MODEL), F32),
        compiler_params=_cparams("arbitrary", "arbitrary"),
        name="moe_experts",
    )(tile_expert, tile_live, row_token, row_token, t, w1, w3, w2)


def _combine_kernel(pos_ref, nxt_ref, y_hbm, g_ref, h_ref, gm_ref, lng_ref, lnb_ref, o_ref, buf_ref, sem):
    i = pl.program_id(0)
    tm = h_ref.shape[0]
    slot = i % 2

    @pl.when(i == 0)
    def _():
        _gather_start(y_hbm, pos_ref, 0, buf_ref.at[0], 2 * tm, sem.at[0])

    @pl.when(i < pl.num_programs(0) - 1)
    def _():
        _gather_start(y_hbm, nxt_ref, 0, buf_ref.at[1 - slot], 2 * tm, sem.at[1 - slot])

    _gather_wait(y_hbm, buf_ref.at[slot], 2 * tm, sem.at[slot])
    g = g_ref[...]
    f = g[:, 0:1] * buf_ref[slot, 0:tm, :] + g[:, 1:2] * buf_ref[slot, tm:2 * tm, :]
    o_ref[...] = _res_ln(h_ref[...], f, gm_ref[0], lng_ref[...], lnb_ref[...])


def _combine(pos, yb, gates, h, mod, lng, lnb):
    rows = h.shape[0]
    tm = pos.shape[-1] // 2
    nt = rows // tm
    return pl.pallas_call(
        _combine_kernel,
        grid=(nt,),
        in_specs=[pl.BlockSpec((1, 1, 2 * tm), lambda i: (i, 0, 0), memory_space=pltpu.SMEM),
                  pl.BlockSpec((1, 1, 2 * tm), lambda i: (jnp.minimum(i + 1, nt - 1), 0, 0), memory_space=pltpu.SMEM),
                  pl.BlockSpec(memory_space=pl.ANY),
                  pl.BlockSpec((tm, LANES), lambda i: (i, 0)),
                  pl.BlockSpec((tm, D_MODEL), lambda i: (i, 0)),
                  _mod_spec(5, tm),
                  pl.BlockSpec((1, D_MODEL), lambda i: (0, 0)),
                  pl.BlockSpec((1, D_MODEL), lambda i: (0, 0))],
        out_specs=pl.BlockSpec((tm, D_MODEL), lambda i: (i, 0)),
        out_shape=jax.ShapeDtypeStruct((rows, D_MODEL), F32),
        scratch_shapes=[pltpu.VMEM((2, 2 * tm, D_MODEL), F32), pltpu.SemaphoreType.DMA((2,))],
        compiler_params=_cparams("arbitrary"),
        name="moe_combine",
    )(pos, pos, yb, gates, h, mod, lng, lnb)


def _cast_kernel(w_ref, o_ref):
    o_ref[...] = w_ref[...].astype(BF16)


def _to_bf16(w, tr):
    r, c = w.shape
    return pl.pallas_call(
        _cast_kernel,
        grid=(r // tr,),
        in_specs=[pl.BlockSpec((tr, c), lambda i: (i, 0))],
        out_specs=pl.BlockSpec((tr, c), lambda i: (i, 0)),
        out_shape=jax.ShapeDtypeStruct((r, c), BF16),
        compiler_params=_cparams("arbitrary"),
        name="cast_bf16",
    )(w)


def _moe(h, mod, router_w, router_b, w1, w3, w2, lng, lnb):
    n = h.shape[0]
    rw = jnp.zeros((D_MODEL, LANES), BF16).at[:, :N_EXPERTS].set(router_w.astype(BF16))
    rb = jnp.zeros((1, LANES), F32).at[0, :N_EXPERTS].set(router_b)
    t, idx, gates = _route(h, mod, rw, rb)
    expert = idx[:, :TOP_K].reshape(-1)
    onehot = (expert[:, None] == jnp.arange(N_EXPERTS)[None, :]).astype(jnp.int32)
    csum = jnp.cumsum(onehot, axis=0)
    rank = jnp.sum(csum * onehot, axis=1) - 1
    counts = csum[-1]
    padded = (counts + MOE_TILE - 1) // MOE_TILE * MOE_TILE
    pend = jnp.cumsum(padded)
    pos = (pend - padded)[expert] + rank
    n_rows = MOE_TILES * MOE_TILE
    row_token = jnp.zeros((n_rows,), jnp.int32).at[pos].set(jnp.arange(n * TOP_K, dtype=jnp.int32) // TOP_K)
    tile_start = jnp.arange(MOE_TILES) * MOE_TILE
    tile_expert = jnp.minimum(jnp.searchsorted(pend, tile_start, side='right'), N_EXPERTS - 1).astype(jnp.int32)
    tile_live = (tile_start < pend[-1]).astype(jnp.int32)
    yb = _experts(tile_expert, tile_live, row_token.reshape(MOE_TILES, 1, MOE_TILE), t, w1, w3, w2)
    tm = 256
    pos2 = pos.reshape(n // tm, tm, TOP_K).transpose(0, 2, 1).reshape(n // tm, 1, TOP_K * tm).astype(jnp.int32)
    return _combine(pos2, yb, gates, h, mod, lng, lnb)


def _to_colmajor(t):
    b, l, c = t.shape
    return t.reshape(b, l // GRID_W, GRID_W, c).transpose(0, 2, 1, 3).reshape(b, l, c)


def _seq_layout(p_cols, pad):
    c = p_cols.shape[-1]
    lat = _to_colmajor(p_cols[:N_LAT].reshape(BATCH, SEQ, c))
    ctx = p_cols[N_LAT:].reshape(BATCH, CTX_LEN, c)
    if not pad:
        return jnp.concatenate([ctx, lat], axis=1)
    z = jnp.zeros((BATCH, CONV_PAD, c), p_cols.dtype)
    return jnp.concatenate([z, ctx, z, lat, z], axis=1)


def _seq_layout_t(p_cols):
    c = p_cols.shape[-1]
    rows = SEQ // GRID_W
    lat = p_cols[:N_LAT].reshape(BATCH, rows, GRID_W, c).transpose(0, 3, 2, 1).reshape(BATCH, c, SEQ)
    ctx = p_cols[N_LAT:].reshape(BATCH, CTX_LEN, c).transpose(0, 2, 1)
    return jnp.concatenate([ctx, lat], axis=2)


def _token_layout_t(s):
    c = s.shape[1]
    rows = SEQ // GRID_W
    lat = s[:, :, CTX_LEN:].reshape(BATCH, c, GRID_W, rows).transpose(0, 3, 2, 1).reshape(N_LAT, c)
    ctx = s[:, :, :CTX_LEN].transpose(0, 2, 1).reshape(N_CTX, c)
    return jnp.concatenate([lat, ctx], axis=0)


def _mixer(h, mod, w_in_all, layer, conv_rg_w, conv_rg_b, conv_m_w, conv_m_b, rg_wa, rg_ba, rg_wx, rg_bx, rg_lam,
           m_gate_b, m_gn_g, p_rg, p_m, rows_out):
    gate0 = C_M + 4 * W_M
    w_tail = w_in_all[layer, :, gate0:]
    w_gate = jnp.zeros((D_MODEL, LANES), BF16).at[:, :N_GATES].set(w_tail[:, :N_GATES].astype(BF16))
    u, pg = _lnmod(h, mod, w_gate)
    p = _proj(u, w_in_all, layer, gate0)
    p2 = _proj(u, w_tail[None, :, N_GATES:], 0, 2 * D_MODEL)

    wcat = jnp.concatenate([rg_wa[0], rg_wx[0], rg_wa[1], rg_wx[1]], axis=-1).astype(BF16)
    bcat = jnp.stack([rg_ba[0], rg_bx[0], rg_ba[1], rg_bx[1]], axis=0)
    bcat = bcat.reshape(4, RG_BLOCKS, RG_BLOCK).transpose(1, 0, 2).reshape(RG_BLOCKS, 1, 4 * RG_BLOCK)
    cb = conv_rg_b.reshape(1, W_RG)
    h0 = jnp.zeros((BATCH, 2, W_RG), F32)
    yr_c, st = _rg_branch(p, h0, conv_rg_w, cb, wcat, bcat, rg_lam, seq=CTX_LEN, row_block0=N_LAT // CTX_LEN)
    yr, _ = _rg_branch(p, st, conv_rg_w, cb, wcat, bcat, rg_lam, seq=SEQ, row_block0=0)
    if rows_out > N_LAT:
        yr = jnp.concatenate([yr, yr_c], axis=0)

    qk_in = _seq_layout(p[:, COL_Q * D_MODEL:(COL_K + 1) * D_MODEL], True)
    v_seq = _seq_layout_t(p[:, COL_V * D_MODEL:(COL_V + 1) * D_MODEL])
    g_seq = _seq_layout(pg, False)
    qscale = jnp.concatenate([jnp.full((1, W_M), M_HEAD_DIM ** -0.5, F32), jnp.ones((1, W_M), F32)], axis=1)
    qk = _qkconv(qk_in, conv_m_w, conv_m_b.reshape(1, 2 * W_M), qscale)
    gate_b = jnp.zeros((1, LANES), F32).at[0, :N_GATES].set(m_gate_b.reshape(-1))
    h_f, h_b = _mlstm(qk, v_seq, g_seq, gate_b)
    hs = _token_layout_t(h_f + h_b)

    return _merge(yr, hs, p, p2, m_gn_g.reshape(1, W_M), p_rg.astype(BF16), p_m.astype(BF16), rows_out)


def kernel(x, c, ctx, c_ctx, w_mod, b_mod, w_in, conv_rg_w, conv_rg_b, conv_m_w, conv_m_b, rg_wa, rg_ba, rg_wx, rg_bx, rg_lam, m_gate_b, m_gn_g, p_rg, p_m, w_out, ln_g, ln_b, ff_w1, ff_w3, ff_w2, router_w, router_b, ex_w1, ex_w3, ex_w2):
    h = jnp.concatenate([x.reshape(N_LAT, D_MODEL), ctx.reshape(N_CTX, D_MODEL)], axis=0)
    cond = jnp.zeros((SUBLANES, D_MODEL), F32).at[:BATCH].set(c).at[BATCH].set(c_ctx)
    for l in range(DEPTH):
        last = l == DEPTH - 1
        rows = N_LAT if last else N_TOK
        mod = _modvec(cond, w_mod, b_mod[l], l)
        mix = _mixer(h, mod, w_in, l, conv_rg_w[l], conv_rg_b[l], conv_m_w[l], conv_m_b[l], rg_wa[l], rg_ba[l],
                     rg_wx[l], rg_bx[l], rg_lam[l], m_gate_b[l], m_gn_g[l], p_rg[l], p_m[l], rows)
        lng = ln_g[l].reshape(2, 1, D_MODEL)
        lnb = ln_b[l].reshape(2, 1, D_MODEL)
        h = _outproj(mix, w_out[l].astype(BF16), h, mod, 2, lng[0], lnb[0], rows)
        k = l // 2
        if l % 2 == 0:
            h = _ffn(h, mod, _to_bf16(ff_w1[k], 256), _to_bf16(ff_w3[k], 256), _to_bf16(ff_w2[k], 1024),
                     lng[1], lnb[1])
        else:
            w1 = _to_bf16(ex_w1[k].reshape(N_EXPERTS * D_MODEL, D_FF), 256).reshape(N_EXPERTS, D_MODEL, D_FF)
            w3 = _to_bf16(ex_w3[k].reshape(N_EXPERTS * D_MODEL, D_FF), 256).reshape(N_EXPERTS, D_MODEL, D_FF)
            w2 = _to_bf16(ex_w2[k].reshape(N_EXPERTS * D_FF, D_MODEL), 1024).reshape(N_EXPERTS, D_FF, D_MODEL)
            h = _moe(h, mod, router_w[k], router_b[k], w1, w3, w2, lng[1], lnb[1])
    return h[:N_LAT].reshape(BATCH, SEQ, D_MODEL)
```

```python
import functools

import jax
import jax.numpy as jnp
from jax import lax
from jax.experimental import pallas as pl
from jax.experimental.pallas import tpu as pltpu

F32 = jnp.float32
BF16 = jnp.bfloat16

D_MODEL = 2048
BATCH = 4
SEQ = 4096
DEPTH = 2
GRID_W = 64
CTX_LEN = 256
W_RG = 2048
RG_BLOCKS = 16
RG_BLOCK = W_RG // RG_BLOCKS
RG_C = 8.0
CONV_W = 4
M_HEADS = 8
M_HEAD_DIM = 256
W_M = M_HEADS * M_HEAD_DIM
M_CHUNK = 128
C_M = 2 * W_RG
N_GATES = 4 * M_HEADS
D_FF = 7168
N_EXPERTS = 8
TOP_K = 2
ALPHA = (2.0 * DEPTH) ** 0.25
LN_EPS = 1e-6

N_LAT = BATCH * SEQ
N_CTX = BATCH * CTX_LEN
N_TOK = N_LAT + N_CTX
LANES = 128
SUBLANES = 8
VMEM_LIMIT = 52 * 1024 * 1024

COL_XR, COL_RGG, COL_Q, COL_K, COL_V, COL_O = range(6)

SEQ_ALL = CTX_LEN + SEQ
CONV_PAD = 8
YS_PITCH = GRID_W + SUBLANES
MOE_TILE = 512
MOE_TILES = (N_LAT * TOP_K) // MOE_TILE + N_EXPERTS


def _cparams(*sem):
    return pltpu.CompilerParams(dimension_semantics=sem, vmem_limit_bytes=VMEM_LIMIT)


def _layer_norm(x):
    mu = jnp.mean(x, axis=-1, keepdims=True)
    xc = x - mu
    var = jnp.mean(xc * xc, axis=-1, keepdims=True)
    return xc * lax.rsqrt(var + LN_EPS)


def _sigmoid(x):
    return 0.5 * jnp.tanh(0.5 * x) + 0.5


def _softplus(x):
    return jnp.maximum(x, 0.0) + jnp.log1p(jnp.exp(-jnp.abs(x)))


def _mod_row(i, tm):
    return jnp.where(i >= N_LAT // tm, BATCH, i // (SEQ // tm))


def _mod_spec(chunk, tm):
    return pl.BlockSpec((1, 1, D_MODEL), lambda i, *_: (_mod_row(i, tm), 0, chunk))


def _modvec_kernel(s_ref, w_ref, b_ref, o_ref):
    s = s_ref[...]
    s = s * _sigmoid(s)
    o_ref[...] = jnp.dot(s.astype(BF16), w_ref[...].astype(BF16), preferred_element_type=F32) + b_ref[...]


def _modvec(cond, w, b, layer):
    n = w.shape[-1]
    tn = 1024
    out = pl.pallas_call(
        _modvec_kernel,
        grid=(n // tn,),
        in_specs=[pl.BlockSpec((SUBLANES, D_MODEL), lambda j: (0, 0)),
                  pl.BlockSpec((None, D_MODEL, tn), lambda j: (layer, 0, j)),
                  pl.BlockSpec((1, tn), lambda j: (0, j))],
        out_specs=pl.BlockSpec((SUBLANES, tn), lambda j: (0, j)),
        out_shape=jax.ShapeDtypeStruct((SUBLANES, n), F32),
        compiler_params=_cparams("arbitrary"),
        name="modvec",
    )(cond, w, b.reshape(1, n))
    return out.reshape(SUBLANES, 1, n)


def _lnmod_kernel(x_ref, sh_ref, sc_ref, wg_ref, u_ref, g_ref):
    u = (_layer_norm(x_ref[...]) * (1.0 + sc_ref[0]) + sh_ref[0]).astype(BF16)
    u_ref[...] = u
    g_ref[...] = jnp.dot(u, wg_ref[...], preferred_element_type=F32)


def _lnmod(h, mod, wg):
    r = h.shape[0]
    tm = 512
    return pl.pallas_call(
        _lnmod_kernel,
        grid=(r // tm,),
        in_specs=[pl.BlockSpec((tm, D_MODEL), lambda i: (i, 0)),
                  _mod_spec(0, tm), _mod_spec(1, tm),
                  pl.BlockSpec((D_MODEL, LANES), lambda i: (0, 0))],
        out_specs=[pl.BlockSpec((tm, D_MODEL), lambda i: (i, 0)),
                   pl.BlockSpec((tm, LANES), lambda i: (i, 0))],
        out_shape=[jax.ShapeDtypeStruct((r, D_MODEL), BF16), jax.ShapeDtypeStruct((r, LANES), F32)],
        compiler_params=_cparams("arbitrary"),
        name="ln_mod",
    )(h, mod, mod, wg)


def _proj_kernel(u_ref, w_ref, o_ref, wb_ref):
    @pl.when(pl.program_id(1) == 0)
    def _():
        wb_ref[...] = w_ref[...].astype(BF16)

    o_ref[...] = jnp.dot(u_ref[...], wb_ref[...], preferred_element_type=F32)


def _proj(u, w, layer, n):
    r = u.shape[0]
    tm, tn = 1024, 1024
    return pl.pallas_call(
        _proj_kernel,
        grid=(n // tn, r // tm),
        in_specs=[pl.BlockSpec((tm, D_MODEL), lambda j, i: (i, 0)),
                  pl.BlockSpec((None, D_MODEL, tn), lambda j, i: (layer, 0, j))],
        out_specs=pl.BlockSpec((tm, tn), lambda j, i: (i, j)),
        out_shape=jax.ShapeDtypeStruct((r, n), F32),
        scratch_shapes=[pltpu.VMEM((D_MODEL, tn), BF16)],
        compiler_params=_cparams("arbitrary", "arbitrary"),
        name="proj_in",
    )(u, w)


def _rg_kernel(x_ref, g_ref, cw_ref, cb_ref, w_ref, bias_ref, lam_ref, h0_ref, y_ref, hl_ref,
               xs, a0, b0, a1, b1, *ys, seq, tc, seg, pitch, colmajor):
    n_rows = seq // GRID_W
    zeros8 = jnp.zeros((CONV_PAD, LANES), F32)
    xs[0:CONV_PAD, :] = zeros8
    xs[CONV_PAD + seq:CONV_PAD + seq + CONV_PAD, :] = zeros8
    if colmajor:
        def perm_in(r, carry):
            for c0 in range(0, GRID_W, SUBLANES):
                dst = pl.multiple_of(CONV_PAD + r * GRID_W + c0, SUBLANES)
                xs[pl.ds(dst, SUBLANES), :] = x_ref[pl.ds(c0 * n_rows + r, SUBLANES, stride=n_rows), :]
            return carry

        lax.fori_loop(0, n_rows, perm_in, 0)
    else:
        xs[CONV_PAD:CONV_PAD + seq, :] = x_ref[...]

    cw = cw_ref[...]
    sp = _softplus(-lam_ref[...])
    dec = ((a0, b0), (a1, b1))
    for c in range(seq // tc):
        r0 = c * tc
        xc = cb_ref[...] + cw[0:1, :] * xs[r0 + CONV_PAD - 2:r0 + CONV_PAD - 2 + tc, :]
        for j in range(1, CONV_W):
            xc = xc + cw[j:j + 1, :] * xs[r0 + CONV_PAD - 2 + j:r0 + CONV_PAD - 2 + j + tc, :]
        z = jnp.dot(xc.astype(BF16), w_ref[...], preferred_element_type=F32) + bias_ref[...]
        for d in range(2):
            rg = _sigmoid(z[:, (2 * d) * LANES:(2 * d + 1) * LANES])
            ig = _sigmoid(z[:, (2 * d + 1) * LANES:(2 * d + 2) * LANES])
            log_a = (-RG_C) * rg * sp[d:d + 1, :]
            a = jnp.exp(log_a)
            b = jnp.sqrt(jnp.tanh(-log_a) * (1.0 + a * a)) * (ig * xc)
            a_ref, b_ref = dec[d]
            for p in range(max(tc // seg, 1)):
                n = min(tc, seg)
                t = r0 + p * n
                dst = (t // seg) * pitch + t % seg
                a_ref[dst:dst + n, :] = a[p * n:(p + 1) * n, :]
                b_ref[dst:dst + n, :] = b[p * n:(p + 1) * n, :]

    def scan_step(i, carry):
        hf, pf, hb, pb = carry
        tf = i
        tb = seg - 1 - i
        af = a0[pl.ds(tf, SUBLANES, stride=pitch), :]
        bf = b0[pl.ds(tf, SUBLANES, stride=pitch), :]
        ab = a1[pl.ds(tb, SUBLANES, stride=pitch), :]
        bb = b1[pl.ds(tb, SUBLANES, stride=pitch), :]
        hf = af * hf + bf
        pf = pf * af
        hb = ab * hb + bb
        pb = pb * ab
        b0[pl.ds(tf, SUBLANES, stride=pitch), :] = hf
        a0[pl.ds(tf, SUBLANES, stride=pitch), :] = pf
        b1[pl.ds(tb, SUBLANES, stride=pitch), :] = hb
        a1[pl.ds(tb, SUBLANES, stride=pitch), :] = pb
        return hf, pf, hb, pb

    z8 = jnp.zeros((SUBLANES, LANES), F32)
    o8 = jnp.ones((SUBLANES, LANES), F32)
    hf, pf, hb, pb = lax.fori_loop(0, seg, scan_step, (z8, o8, z8, o8), unroll=4)

    h0 = h0_ref[...]
    cf = [None] * SUBLANES
    cbk = [None] * SUBLANES
    carry = h0[0:1, :]
    for s in range(SUBLANES):
        cf[s] = carry
        carry = pf[s:s + 1, :] * carry + hf[s:s + 1, :]
    hl_f = carry
    carry = h0[1:2, :]
    for s in reversed(range(SUBLANES)):
        cbk[s] = carry
        carry = pb[s:s + 1, :] * carry + hb[s:s + 1, :]
    hl_b = carry
    hl_ref[...] = jnp.concatenate([hl_f, hl_b], axis=0)

    for s in range(SUBLANES):
        for c in range(max(seg // tc, 1)):
            n = min(tc, seg)
            src = s * pitch + c * n
            t = s * seg + c * n
            h = (b0[src:src + n, :] + a0[src:src + n, :] * cf[s]
                 + b1[src:src + n, :] + a1[src:src + n, :] * cbk[s])
            if colmajor:
                for k in range(n // GRID_W):
                    dst = (t // GRID_W + k) * YS_PITCH
                    ys[0][dst:dst + GRID_W, :] = h[k * GRID_W:(k + 1) * GRID_W, :]
            else:
                y_ref[t:t + n, :] = (h * jax.nn.gelu(g_ref[t:t + n, :])).astype(y_ref.dtype)

    if colmajor:
        def perm_out(c, carry):
            hcol = ys[0][pl.ds(c, n_rows, stride=YS_PITCH), :]
            j0 = pl.multiple_of(c * n_rows, n_rows)
            y_ref[pl.ds(j0, n_rows), :] = (hcol * jax.nn.gelu(g_ref[pl.ds(j0, n_rows), :])).astype(y_ref.dtype)
            return carry

        lax.fori_loop(0, GRID_W, perm_out, 0)


def _rg_branch(p, h0, cw, cb, wcat, bcat, lam, *, seq, row_block0, colmajor):
    tc = min(256, seq)
    seg = seq // SUBLANES
    pitch = seg + SUBLANES if (seg // SUBLANES) % 2 == 0 else seg
    nblk = W_RG // LANES
    rows = SUBLANES * pitch
    kern = functools.partial(_rg_kernel, seq=seq, tc=tc, seg=seg, pitch=pitch, colmajor=colmajor)
    ys_scratch = [pltpu.VMEM(((seq // GRID_W) * YS_PITCH, LANES), F32)] if colmajor else []
    in_specs = [
        pl.BlockSpec((seq, LANES), lambda b, c: (row_block0 + b, COL_XR * nblk + c)),
        pl.BlockSpec((seq, LANES), lambda b, c: (row_block0 + b, COL_RGG * nblk + c)),
        pl.BlockSpec((CONV_W, LANES), lambda b, c: (0, c)),
        pl.BlockSpec((1, LANES), lambda b, c: (0, c)),
        pl.BlockSpec((None, LANES, 4 * LANES), lambda b, c: (c, 0, 0)),
        pl.BlockSpec((None, 1, 4 * LANES), lambda b, c: (c, 0, 0)),
        pl.BlockSpec((2, LANES), lambda b, c: (0, c)),
        pl.BlockSpec((None, 2, LANES), lambda b, c: (b, 0, c)),
    ]
    out_shape = [jax.ShapeDtypeStruct((BATCH * seq, W_RG), BF16), jax.ShapeDtypeStruct((BATCH, 2, W_RG), F32)]
    out_specs = [pl.BlockSpec((seq, LANES), lambda b, c: (b, c)),
                 pl.BlockSpec((None, 2, LANES), lambda b, c: (b, 0, c))]
    return pl.pallas_call(
        kern,
        grid=(BATCH, nblk),
        in_specs=in_specs,
        out_specs=out_specs,
        out_shape=out_shape,
        scratch_shapes=([pltpu.VMEM((seq + 2 * CONV_PAD, LANES), F32)] + [pltpu.VMEM((rows, LANES), F32)] * 4
                        + ys_scratch),
        compiler_params=_cparams("arbitrary", "arbitrary"),
        name=f"rglru_{seq}",
    )(p, p, cw, cb, wcat, bcat, lam, h0)


def _qkconv_kernel(lat_ref, ctx_ref, cw_ref, cb_ref, sc_ref, o_ref, xs, *, tc):
    cw = cw_ref[...]
    zeros = jnp.zeros((CONV_PAD, xs.shape[1]), F32)
    xs[0:CONV_PAD, :] = zeros
    for src_ref, out_off, n in ((ctx_ref, 0, CTX_LEN), (lat_ref, CTX_LEN, SEQ)):
        xs[CONV_PAD:CONV_PAD + n, :] = src_ref[...]
        xs[CONV_PAD + n:CONV_PAD + n + CONV_PAD, :] = zeros
        for c in range(n // tc):
            r0 = CONV_PAD + c * tc - 2
            xc = cb_ref[...] + cw[0:1, :] * xs[r0:r0 + tc, :]
            for j in range(1, CONV_W):
                xc = xc + cw[j:j + 1, :] * xs[r0 + j:r0 + j + tc, :]
            y = xc * _sigmoid(xc) * sc_ref[...]
            o_ref[out_off + c * tc:out_off + (c + 1) * tc, :] = y.astype(o_ref.dtype)


def _qkconv(p, cw, cb, scale):
    cwid = 256
    n = 2 * W_M
    col0 = COL_Q * D_MODEL // cwid
    return pl.pallas_call(
        functools.partial(_qkconv_kernel, tc=256),
        grid=(BATCH, n // cwid),
        in_specs=[pl.BlockSpec((SEQ, cwid), lambda b, c: (b, col0 + c)),
                  pl.BlockSpec((CTX_LEN, cwid), lambda b, c: (N_LAT // CTX_LEN + b, col0 + c)),
                  pl.BlockSpec((CONV_W, cwid), lambda b, c: (0, c)),
                  pl.BlockSpec((1, cwid), lambda b, c: (0, c)),
                  pl.BlockSpec((1, cwid), lambda b, c: (0, c))],
        out_specs=pl.BlockSpec((None, SEQ_ALL, cwid), lambda b, c: (b, 0, c)),
        out_shape=jax.ShapeDtypeStruct((BATCH, SEQ_ALL, n), BF16),
        scratch_shapes=[pltpu.VMEM((SEQ + 2 * CONV_PAD, cwid), F32)],
        compiler_params=_cparams("arbitrary", "arbitrary"),
        name="qk_conv",
    )(p, p, cw, cb, scale)


def _lane_cumsum(x, reverse):
    lane = lax.broadcasted_iota(jnp.int32, x.shape, 1)
    s = 1
    while s < LANES:
        if reverse:
            x = x + jnp.where(lane < LANES - s, pltpu.roll(x, LANES - s, axis=1), 0.0)
        else:
            x = x + jnp.where(lane >= s, pltpu.roll(x, s, axis=1), 0.0)
        s *= 2
    return x


def _mlstm_direction(q_ref, k_ref, v_ref, g, o_ref, c_ref, n_ref, m_ref, d):
    reverse = d == 1
    nt = (((1,), (1,)), ((), ()))
    gt = g.T
    base = d * 2 * M_HEADS
    i_rows = gt[base:base + M_HEADS, :]
    f_rows = -_softplus(-gt[base + M_HEADS:base + 2 * M_HEADS, :])
    bcum = _lane_cumsum(f_rows, reverse)
    r_rows = i_rows - bcum
    pad = jnp.zeros((M_CHUNK - M_HEADS, M_CHUNK), F32)
    r_cols = jnp.concatenate([r_rows, pad], axis=0).T
    last = 0 if reverse else M_CHUNK - 1

    src_i = lax.broadcasted_iota(jnp.int32, (M_CHUNK, M_CHUNK), 0)
    dst_i = lax.broadcasted_iota(jnp.int32, (M_CHUNK, M_CHUNK), 1)
    mask = (src_i >= dst_i) if reverse else (src_i <= dst_i)

    for h in range(M_HEADS):
        sl = slice(h * M_HEAD_DIM, (h + 1) * M_HEAD_DIM)
        qh = q_ref[:, sl]
        kh = k_ref[:, sl]
        vt = v_ref[:, sl].T
        b_row = bcum[h:h + 1, :]
        m_prev = m_ref[d, h:h + 1, :]
        b_last = jnp.broadcast_to(b_row[:, last:last + 1], (1, M_CHUNK))

        log_d = jnp.where(mask, b_row + r_cols[:, h:h + 1], -jnp.inf)
        log_inter = b_row + m_prev
        m_t = jnp.maximum(jnp.max(log_d, axis=0, keepdims=True), log_inter)
        st = lax.dot_general(kh, qh, nt, preferred_element_type=F32)
        st = st * jnp.exp(log_d - m_t)
        w_inter = jnp.exp(log_inter - m_t)
        c_prev = c_ref[d, h]
        n_prev = n_ref[d, h]
        cq = lax.dot_general(c_prev.astype(BF16), qh, nt, preferred_element_type=F32)
        num = jnp.dot(vt.astype(BF16), st.astype(BF16), preferred_element_type=F32) + w_inter * cq
        qn = lax.dot_general(n_prev.astype(BF16), qh, nt, preferred_element_type=F32)[0:1, :]
        den = jnp.sum(st, axis=0, keepdims=True) + w_inter * qn
        o_ref[:, sl] = (num / jnp.maximum(jnp.abs(den), jnp.exp(-m_t))).T

        log_w = b_last - b_row + i_rows[h:h + 1, :]
        m_new = jnp.maximum(b_last + m_prev, jnp.max(log_w, axis=1, keepdims=True))
        w = jnp.exp(log_w - m_new)
        decay = jnp.exp(b_last + m_prev - m_new)
        vw = (vt * w).astype(BF16)
        c_ref[d, h] = decay[:, 0:1] * c_prev + jnp.dot(vw, kh, preferred_element_type=F32)
        w8 = jnp.broadcast_to(w, (SUBLANES, M_CHUNK)).astype(BF16)
        n_ref[d, h] = decay[:, 0:1] * n_prev + jnp.dot(w8, kh, preferred_element_type=F32)
        m_ref[d, h:h + 1, :] = m_new


def _mlstm_kernel(qf_ref, kf_ref, vf_ref, gf_ref, qb_ref, kb_ref, vb_ref, gb_ref, bias_ref, of_ref, ob_ref,
                  c_ref, n_ref, m_ref):
    @pl.when(pl.program_id(1) == 0)
    def _():
        c_ref[...] = jnp.zeros(c_ref.shape, F32)
        n_ref[...] = jnp.zeros(n_ref.shape, F32)
        m_ref[...] = jnp.zeros(m_ref.shape, F32)

    bias = bias_ref[...]
    _mlstm_direction(qf_ref, kf_ref, vf_ref, gf_ref[...] + bias, of_ref, c_ref, n_ref, m_ref, 0)
    _mlstm_direction(qb_ref, kb_ref, vb_ref, gb_ref[...] + bias, ob_ref, c_ref, n_ref, m_ref, 1)


def _chunk_order(i, reverse):
    if not reverse:
        return i
    nctx = CTX_LEN // M_CHUNK
    ntot = SEQ_ALL // M_CHUNK
    return jnp.where(i < nctx, nctx - 1 - i, ntot + nctx - 1 - i)


def _token_chunk(b, i, reverse):
    nctx = CTX_LEN // M_CHUNK
    ci = _chunk_order(i, reverse)
    return jnp.where(ci < nctx, N_LAT // M_CHUNK + b * nctx + ci, b * (SEQ // M_CHUNK) + ci - nctx)


def _mlstm(q_and_k, p, g, gate_b):
    def blk(col, reverse):
        return pl.BlockSpec((None, M_CHUNK, W_M), lambda b, i: (b, _chunk_order(i, reverse), col))

    def tok(col, reverse, width=W_M):
        return pl.BlockSpec((M_CHUNK, width), lambda b, i: (_token_chunk(b, i, reverse), col))

    in_specs = [blk(0, False), blk(1, False), tok(COL_V, False), tok(0, False, LANES),
                blk(0, True), blk(1, True), tok(COL_V, True), tok(0, True, LANES),
                pl.BlockSpec((1, LANES), lambda b, i: (0, 0))]
    out = jax.ShapeDtypeStruct((N_TOK, W_M), F32)
    return pl.pallas_call(
        _mlstm_kernel,
        grid=(BATCH, SEQ_ALL // M_CHUNK),
        in_specs=in_specs,
        out_specs=[tok(0, False), tok(0, True)],
        out_shape=[out, out],
        scratch_shapes=[pltpu.VMEM((2, M_HEADS, M_HEAD_DIM, M_HEAD_DIM), F32),
                        pltpu.VMEM((2, M_HEADS, SUBLANES, M_HEAD_DIM), F32),
                        pltpu.VMEM((2, M_HEADS, LANES), F32)],
        compiler_params=_cparams("arbitrary", "arbitrary"),
        name="mlstm",
    )(q_and_k, q_and_k, p, g, q_and_k, q_and_k, p, g, gate_b)


def _merge_kernel(yr_ref, hf_ref, hb_ref, o_ref, ga_ref, gb_ref, gn_ref, prg_ref, pm_ref, mix_ref):
    hs = hf_ref[...] + hb_ref[...]
    hn = jnp.concatenate([_layer_norm(hs[:, h * M_HEAD_DIM:(h + 1) * M_HEAD_DIM]) for h in range(M_HEADS)], axis=1)
    ym = (hn * gn_ref[...] * _sigmoid(o_ref[...])).astype(BF16)
    a = jnp.dot(yr_ref[...], prg_ref[...], preferred_element_type=F32)
    b = jnp.dot(ym, pm_ref[...], preferred_element_type=F32)
    mix_ref[...] = (_sigmoid(ga_ref[...]) * a + _sigmoid(gb_ref[...]) * b).astype(BF16)


def _merge(yr, hf, hb, p1, p2, gn, p_rg, p_m, rows):
    tm, tn = 256, 1024
    nj = D_MODEL // tn
    return pl.pallas_call(
        _merge_kernel,
        grid=(nj, rows // tm),
        in_specs=[pl.BlockSpec((tm, W_RG), lambda j, i: (i, 0)),
                  pl.BlockSpec((tm, W_M), lambda j, i: (i, 0)),
                  pl.BlockSpec((tm, W_M), lambda j, i: (i, 0)),
                  pl.BlockSpec((tm, W_M), lambda j, i: (i, COL_O)),
                  pl.BlockSpec((tm, tn), lambda j, i: (i, j)),
                  pl.BlockSpec((tm, tn), lambda j, i: (i, nj + j)),
                  pl.BlockSpec((1, W_M), lambda j, i: (0, 0)),
                  pl.BlockSpec((W_RG, tn), lambda j, i: (0, j)),
                  pl.BlockSpec((W_M, tn), lambda j, i: (0, j))],
        out_specs=pl.BlockSpec((tm, tn), lambda j, i: (i, j)),
        out_shape=jax.ShapeDtypeStruct((rows, D_MODEL), BF16),
        compiler_params=_cparams("arbitrary", "arbitrary"),
        name="merge",
    )(yr, hf, hb, p1, p2, p2, gn, p_rg, p_m)


def _res_ln(h, f, gate, lng, lnb):
    y = _layer_norm(ALPHA * h + gate * f)
    return y * lng + lnb


def _outproj_kernel(a_ref, w_ref, h_ref, g_ref, lng_ref, lnb_ref, o_ref):
    f = jnp.dot(a_ref[...], w_ref[...], preferred_element_type=F32)
    o_ref[...] = _res_ln(h_ref[...], f, g_ref[0], lng_ref[...], lnb_ref[...])


def _outproj(a, w, h, mod, gate_chunk, lng, lnb, rows):
    tm = 256
    return pl.pallas_call(
        _outproj_kernel,
        grid=(rows // tm,),
        in_specs=[pl.BlockSpec((tm, D_MODEL), lambda i: (i, 0)),
                  pl.BlockSpec((D_MODEL, D_MODEL), lambda i: (0, 0)),
                  pl.BlockSpec((tm, D_MODEL), lambda i: (i, 0)),
                  _mod_spec(gate_chunk, tm),
                  pl.BlockSpec((1, D_MODEL), lambda i: (0, 0)),
                  pl.BlockSpec((1, D_MODEL), lambda i: (0, 0))],
        out_specs=pl.BlockSpec((tm, D_MODEL), lambda i: (i, 0)),
        out_shape=jax.ShapeDtypeStruct((rows, D_MODEL), F32),
        compiler_params=_cparams("arbitrary"),
        name="out_proj",
    )(a, w, h, mod, lng, lnb)


def _ffn_kernel(h_ref, sh_ref, sc_ref, g_ref, w1_ref, w3_ref, w2_ref, lng_ref, lnb_ref, o_ref, u_ref, acc_ref):
    f = pl.program_id(1)

    @pl.when(f == 0)
    def _():
        u = _layer_norm(h_ref[...]) * (1.0 + sc_ref[0]) + sh_ref[0]
        u_ref[...] = u.astype(BF16)
        acc_ref[...] = jnp.zeros(acc_ref.shape, F32)

    u = u_ref[...]
    h1 = jnp.dot(u, w1_ref[...], preferred_element_type=F32)
    h3 = jnp.dot(u, w3_ref[...], preferred_element_type=F32)
    act = (h1 * _sigmoid(h1) * h3).astype(BF16)
    acc_ref[...] += jnp.dot(act, w2_ref[...], preferred_element_type=F32)

    @pl.when(f == pl.num_programs(1) - 1)
    def _():
        o_ref[...] = _res_ln(h_ref[...], acc_ref[...], g_ref[0], lng_ref[...], lnb_ref[...])


def _ffn(h, mod, w1, w3, w2, lng, lnb):
    rows = h.shape[0]
    tm, tf = 512, 512
    return pl.pallas_call(
        _ffn_kernel,
        grid=(rows // tm, D_FF // tf),
        in_specs=[pl.BlockSpec((tm, D_MODEL), lambda i, f: (i, 0)),
                  _mod_spec(3, tm), _mod_spec(4, tm), _mod_spec(5, tm),
                  pl.BlockSpec((D_MODEL, tf), lambda i, f: (0, f)),
                  pl.BlockSpec((D_MODEL, tf), lambda i, f: (0, f)),
                  pl.BlockSpec((tf, D_MODEL), lambda i, f: (f, 0)),
                  pl.BlockSpec((1, D_MODEL), lambda i, f: (0, 0)),
                  pl.BlockSpec((1, D_MODEL), lambda i, f: (0, 0))],
        out_specs=pl.BlockSpec((tm, D_MODEL), lambda i, f: (i, 0)),
        out_shape=jax.ShapeDtypeStruct((rows, D_MODEL), F32),
        scratch_shapes=[pltpu.VMEM((tm, D_MODEL), BF16), pltpu.VMEM((tm, D_MODEL), F32)],
        compiler_params=_cparams("arbitrary", "arbitrary"),
        name="ffn_dense",
    )(h, mod, mod, mod, w1, w3, w2, lng, lnb)


def _route_kernel(h_ref, sh_ref, sc_ref, rw_ref, rb_ref, t_ref, idx_ref, gate_ref):
    u = _layer_norm(h_ref[...]) * (1.0 + sc_ref[0]) + sh_ref[0]
    u = u.astype(BF16)
    t_ref[...] = u.astype(F32)
    lane = lax.broadcasted_iota(jnp.int32, (h_ref.shape[0], LANES), 1)
    logits = jnp.dot(u, rw_ref[...], preferred_element_type=F32) + rb_ref[...]
    logits = jnp.where(lane < N_EXPERTS, logits, -jnp.inf)
    v1 = jnp.max(logits, axis=-1, keepdims=True)
    i1 = jnp.min(jnp.where(logits == v1, lane, LANES), axis=-1, keepdims=True)
    rest = jnp.where(lane == i1, -jnp.inf, logits)
    v2 = jnp.max(rest, axis=-1, keepdims=True)
    i2 = jnp.min(jnp.where(rest == v2, lane, LANES), axis=-1, keepdims=True)
    e2 = jnp.exp(v2 - v1)
    g1 = 1.0 / (1.0 + e2)
    idx_ref[...] = jnp.where(lane == 0, i1, jnp.where(lane == 1, i2, 0))
    gate_ref[...] = jnp.where(lane == 0, g1, jnp.where(lane == 1, e2 * g1, 0.0))


def _route(h, mod, rw, rb):
    rows = h.shape[0]
    tm = 512
    return pl.pallas_call(
        _route_kernel,
        grid=(rows // tm,),
        in_specs=[pl.BlockSpec((tm, D_MODEL), lambda i: (i, 0)),
                  _mod_spec(3, tm), _mod_spec(4, tm),
                  pl.BlockSpec((D_MODEL, LANES), lambda i: (0, 0)),
                  pl.BlockSpec((1, LANES), lambda i: (0, 0))],
        out_specs=[pl.BlockSpec((tm, D_MODEL), lambda i: (i, 0)),
                   pl.BlockSpec((tm, LANES), lambda i: (i, 0)),
                   pl.BlockSpec((tm, LANES), lambda i: (i, 0))],
        out_shape=[jax.ShapeDtypeStruct((rows, D_MODEL), F32), jax.ShapeDtypeStruct((rows, LANES), jnp.int32),
                   jax.ShapeDtypeStruct((rows, LANES), F32)],
        compiler_params=_cparams("arbitrary"),
        name="moe_route",
    )(h, mod, mod, rw, rb)


def _gather_start(src_hbm, idx_ref, idx0, dst_ref, n, sem):
    def issue(r, carry):
        row = idx_ref[0, 0, idx0 + r]
        pltpu.make_async_copy(src_hbm.at[pl.ds(row, 1)], dst_ref.at[pl.ds(r, 1)], sem).start()
        return carry

    lax.fori_loop(0, n, issue, 0)


def _gather_wait(src_hbm, dst_ref, n, sem):
    pltpu.make_async_copy(src_hbm.at[pl.ds(0, n)], dst_ref, sem).wait()


def _experts_kernel(te_ref, tv_ref, tok_ref, nxt_ref, t_hbm, w1_ref, w3_ref, w2_ref, o_ref,
                    xf_ref, x_ref, acc_ref, sem):
    i = pl.program_id(0)
    f = pl.program_id(1)
    last_tile = pl.num_programs(0) - 1
    live = tv_ref[i] > 0
    next_live = jnp.logical_and(i < last_tile, tv_ref[jnp.minimum(i + 1, last_tile)] > 0)
    slot = i % 2

    @pl.when(jnp.logical_and(live, jnp.logical_and(f == 0, i == 0)))
    def _():
        _gather_start(t_hbm, tok_ref, 0, xf_ref.at[0], MOE_TILE, sem.at[0])

    @pl.when(jnp.logical_and(live, f == 0))
    def _():
        _gather_wait(t_hbm, xf_ref.at[slot], MOE_TILE, sem.at[slot])
        x_ref[...] = xf_ref[slot].astype(BF16)
        acc_ref[...] = jnp.zeros(acc_ref.shape, F32)

    @pl.when(jnp.logical_and(next_live, f == 0))
    def _():
        _gather_start(t_hbm, nxt_ref, 0, xf_ref.at[1 - slot], MOE_TILE, sem.at[1 - slot])

    @pl.when(live)
    def _():
        x = x_ref[...]
        h1 = jnp.dot(x, w1_ref[...], preferred_element_type=F32)
        h3 = jnp.dot(x, w3_ref[...], preferred_element_type=F32)
        act = (h1 * _sigmoid(h1) * h3).astype(BF16)
        acc_ref[...] += jnp.dot(act, w2_ref[...], preferred_element_type=F32)

    @pl.when(f == pl.num_programs(1) - 1)
    def _():
        o_ref[...] = jnp.where(live, acc_ref[...], 0.0)


def _experts(tile_expert, tile_live, row_token, t, w1, w3, w2):
    tf = 512
    grid_spec = pltpu.PrefetchScalarGridSpec(
        num_scalar_prefetch=2,
        grid=(MOE_TILES, D_FF // tf),
        in_specs=[pl.BlockSpec((1, 1, MOE_TILE), lambda i, f, te, tv: (i, 0, 0), memory_space=pltpu.SMEM),
                  pl.BlockSpec((1, 1, MOE_TILE), lambda i, f, te, tv: (jnp.minimum(i + 1, MOE_TILES - 1), 0, 0),
                               memory_space=pltpu.SMEM),
                  pl.BlockSpec(memory_space=pl.ANY),
                  pl.BlockSpec((None, D_MODEL, tf), lambda i, f, te, tv: (te[i], 0, f)),
                  pl.BlockSpec((None, D_MODEL, tf), lambda i, f, te, tv: (te[i], 0, f)),
                  pl.BlockSpec((None, tf, D_MODEL), lambda i, f, te, tv: (te[i], f, 0))],
        out_specs=pl.BlockSpec((MOE_TILE, D_MODEL), lambda i, f, te, tv: (i, 0)),
        scratch_shapes=[pltpu.VMEM((2, MOE_TILE, D_MODEL), F32), pltpu.VMEM((MOE_TILE, D_MODEL), BF16),
                        pltpu.VMEM((MOE_TILE, D_MODEL), F32), pltpu.SemaphoreType.DMA((2,))],
    )
    return pl.pallas_call(
        _experts_kernel,
        grid_spec=grid_spec,
        out_shape=jax.ShapeDtypeStruct((MOE_TILES * MOE_TILE, D_MODEL), F32),
        compiler_params=_cparams("arbitrary", "arbitrary"),
        name="moe_experts",
    )(tile_expert, tile_live, row_token, row_token, t, w1, w3, w2)


def _combine_kernel(pos_ref, nxt_ref, y_hbm, g_ref, h_ref, gm_ref, lng_ref, lnb_ref, o_ref, buf_ref, sem):
    i = pl.program_id(0)
    tm = h_ref.shape[0]
    slot = i % 2

    @pl.when(i == 0)
    def _():
        _gather_start(y_hbm, pos_ref, 0, buf_ref.at[0], 2 * tm, sem.at[0])

    @pl.when(i < pl.num_programs(0) - 1)
    def _():
        _gather_start(y_hbm, nxt_ref, 0, buf_ref.at[1 - slot], 2 * tm, sem.at[1 - slot])

    _gather_wait(y_hbm, buf_ref.at[slot], 2 * tm, sem.at[slot])
    g = g_ref[...]
    f = g[:, 0:1] * buf_ref[slot, 0:tm, :] + g[:, 1:2] * buf_ref[slot, tm:2 * tm, :]
    o_ref[...] = _res_ln(h_ref[...], f, gm_ref[0], lng_ref[...], lnb_ref[...])


def _combine(pos, yb, gates, h, mod, lng, lnb):
    rows = h.shape[0]
    tm = pos.shape[-1] // 2
    nt = rows // tm
    return pl.pallas_call(
        _combine_kernel,
        grid=(nt,),
        in_specs=[pl.BlockSpec((1, 1, 2 * tm), lambda i: (i, 0, 0), memory_space=pltpu.SMEM),
                  pl.BlockSpec((1, 1, 2 * tm), lambda i: (jnp.minimum(i + 1, nt - 1), 0, 0), memory_space=pltpu.SMEM),
                  pl.BlockSpec(memory_space=pl.ANY),
                  pl.BlockSpec((tm, LANES), lambda i: (i, 0)),
                  pl.BlockSpec((tm, D_MODEL), lambda i: (i, 0)),
                  _mod_spec(5, tm),
                  pl.BlockSpec((1, D_MODEL), lambda i: (0, 0)),
                  pl.BlockSpec((1, D_MODEL), lambda i: (0, 0))],
        out_specs=pl.BlockSpec((tm, D_MODEL), lambda i: (i, 0)),
        out_shape=jax.ShapeDtypeStruct((rows, D_MODEL), F32),
        scratch_shapes=[pltpu.VMEM((2, 2 * tm, D_MODEL), F32), pltpu.SemaphoreType.DMA((2,))],
        compiler_params=_cparams("arbitrary"),
        name="moe_combine",
    )(pos, pos, yb, gates, h, mod, lng, lnb)


def _cast_kernel(w_ref, o_ref):
    o_ref[...] = w_ref[...].astype(BF16)


def _to_bf16(w, tr):
    r, c = w.shape
    return pl.pallas_call(
        _cast_kernel,
        grid=(r // tr,),
        in_specs=[pl.BlockSpec((tr, c), lambda i: (i, 0))],
        out_specs=pl.BlockSpec((tr, c), lambda i: (i, 0)),
        out_shape=jax.ShapeDtypeStruct((r, c), BF16),
        compiler_params=_cparams("arbitrary"),
        name="cast_bf16",
    )(w)


def _moe(h, mod, router_w, router_b, w1, w3, w2, lng, lnb):
    n = h.shape[0]
    rw = jnp.zeros((D_MODEL, LANES), BF16).at[:, :N_EXPERTS].set(router_w.astype(BF16))
    rb = jnp.zeros((1, LANES), F32).at[0, :N_EXPERTS].set(router_b)
    t, idx, gates = _route(h, mod, rw, rb)
    expert = idx[:, :TOP_K].reshape(-1)
    onehot = (expert[:, None] == jnp.arange(N_EXPERTS)[None, :]).astype(jnp.int32)
    csum = jnp.cumsum(onehot, axis=0)
    rank = jnp.sum(csum * onehot, axis=1) - 1
    counts = csum[-1]
    padded = (counts + MOE_TILE - 1) // MOE_TILE * MOE_TILE
    pend = jnp.cumsum(padded)
    pos = (pend - padded)[expert] + rank
    n_rows = MOE_TILES * MOE_TILE
    row_token = jnp.zeros((n_rows,), jnp.int32).at[pos].set(jnp.arange(n * TOP_K, dtype=jnp.int32) // TOP_K)
    tile_start = jnp.arange(MOE_TILES) * MOE_TILE
    tile_expert = jnp.minimum(jnp.searchsorted(pend, tile_start, side='right'), N_EXPERTS - 1).astype(jnp.int32)
    tile_live = (tile_start < pend[-1]).astype(jnp.int32)
    yb = _experts(tile_expert, tile_live, row_token.reshape(MOE_TILES, 1, MOE_TILE), t, w1, w3, w2)
    tm = 256
    pos2 = pos.reshape(n // tm, tm, TOP_K).transpose(0, 2, 1).reshape(n // tm, 1, TOP_K * tm).astype(jnp.int32)
    return _combine(pos2, yb, gates, h, mod, lng, lnb)


def _swap_grid_order(t):
    b, l, c = t.shape
    return t.reshape(b, l // GRID_W, GRID_W, c).transpose(0, 2, 1, 3).reshape(b, l, c)


def _mixer(h, mod, w_in_all, layer, conv_rg_w, conv_rg_b, conv_m_w, conv_m_b, rg_wa, rg_ba, rg_wx, rg_bx, rg_lam,
           m_gate_b, m_gn_g, p_rg, p_m, rows_out):
    gate0 = C_M + 4 * W_M
    w_tail = w_in_all[layer, :, gate0:]
    w_gate = jnp.zeros((D_MODEL, LANES), BF16).at[:, :N_GATES].set(w_tail[:, :N_GATES].astype(BF16))
    u, pg = _lnmod(h, mod, w_gate)
    p = _proj(u, w_in_all, layer, gate0)
    p2 = _proj(u, w_tail[None, :, N_GATES:], 0, 2 * D_MODEL)

    wcat = jnp.concatenate([rg_wa[0], rg_wx[0], rg_wa[1], rg_wx[1]], axis=-1).astype(BF16)
    bcat = jnp.stack([rg_ba[0], rg_bx[0], rg_ba[1], rg_bx[1]], axis=0)
    bcat = bcat.reshape(4, RG_BLOCKS, RG_BLOCK).transpose(1, 0, 2).reshape(RG_BLOCKS, 1, 4 * RG_BLOCK)
    cb = conv_rg_b.reshape(1, W_RG)
    h0 = jnp.zeros((BATCH, 2, W_RG), F32)
    yr_c, st = _rg_branch(p, h0, conv_rg_w, cb, wcat, bcat, rg_lam, seq=CTX_LEN, row_block0=N_LAT // CTX_LEN,
                          colmajor=False)
    yr, _ = _rg_branch(p, st, conv_rg_w, cb, wcat, bcat, rg_lam, seq=SEQ, row_block0=0, colmajor=True)
    if rows_out > N_LAT:
        yr = jnp.concatenate([yr, yr_c], axis=0)

    qscale = jnp.concatenate([jnp.full((1, W_M), M_HEAD_DIM ** -0.5, F32), jnp.ones((1, W_M), F32)], axis=1)
    qk = _qkconv(p, conv_m_w, conv_m_b.reshape(1, 2 * W_M), qscale)
    gate_b = jnp.zeros((1, LANES), F32).at[0, :N_GATES].set(m_gate_b.reshape(-1))
    h_f, h_b = _mlstm(qk, p, pg, gate_b)

    return _merge(yr, h_f, h_b, p, p2, m_gn_g.reshape(1, W_M), p_rg.astype(BF16), p_m.astype(BF16), rows_out)


def kernel(x, c, ctx, c_ctx, w_mod, b_mod, w_in, conv_rg_w, conv_rg_b, conv_m_w, conv_m_b, rg_wa, rg_ba, rg_wx, rg_bx, rg_lam, m_gate_b, m_gn_g, p_rg, p_m, w_out, ln_g, ln_b, ff_w1, ff_w3, ff_w2, router_w, router_b, ex_w1, ex_w3, ex_w2):
    h = jnp.concatenate([_swap_grid_order(x).reshape(N_LAT, D_MODEL), ctx.reshape(N_CTX, D_MODEL)], axis=0)
    cond = jnp.zeros((SUBLANES, D_MODEL), F32).at[:BATCH].set(c).at[BATCH].set(c_ctx)
    for l in range(DEPTH):
        last = l == DEPTH - 1
        rows = N_LAT if last else N_TOK
        mod = _modvec(cond, w_mod, b_mod[l], l)
        mix = _mixer(h, mod, w_in, l, conv_rg_w[l], conv_rg_b[l], conv_m_w[l], conv_m_b[l], rg_wa[l], rg_ba[l],
                     rg_wx[l], rg_bx[l], rg_lam[l], m_gate_b[l], m_gn_g[l], p_rg[l], p_m[l], rows)
        lng = ln_g[l].reshape(2, 1, D_MODEL)
        lnb = ln_b[l].reshape(2, 1, D_MODEL)
        h = _outproj(mix, w_out[l].astype(BF16), h, mod, 2, lng[0], lnb[0], rows)
        k = l // 2
        if l % 2 == 0:
            h = _ffn(h, mod, _to_bf16(ff_w1[k], 256), _to_bf16(ff_w3[k], 256), _to_bf16(ff_w2[k], 1024),
                     lng[1], lnb[1])
        else:
            w1 = _to_bf16(ex_w1[k].reshape(N_EXPERTS * D_MODEL, D_FF), 256).reshape(N_EXPERTS, D_MODEL, D_FF)
            w3 = _to_bf16(ex_w3[k].reshape(N_EXPERTS * D_MODEL, D_FF), 256).reshape(N_EXPERTS, D_MODEL, D_FF)
            w2 = _to_bf16(ex_w2[k].reshape(N_EXPERTS * D_FF, D_MODEL), 1024).reshape(N_EXPERTS, D_FF, D_MODEL)
            h = _moe(h, mod, router_w[k], router_b[k], w1, w3, w2, lng[1], lnb[1])
    return _swap_grid_order(h[:N_LAT].reshape(BATCH, SEQ, D_MODEL))
```

```python
import functools

import jax
import jax.numpy as jnp
from jax import lax
from jax.experimental import pallas as pl
from jax.experimental.pallas import tpu as pltpu

F32 = jnp.float32
BF16 = jnp.bfloat16

D_MODEL = 2048
BATCH = 4
SEQ = 4096
DEPTH = 2
GRID_W = 64
CTX_LEN = 256
W_RG = 2048
RG_BLOCKS = 16
RG_BLOCK = W_RG // RG_BLOCKS
RG_C = 8.0
CONV_W = 4
M_HEADS = 8
M_HEAD_DIM = 256
W_M = M_HEADS * M_HEAD_DIM
M_CHUNK = 128
C_M = 2 * W_RG
N_GATES = 4 * M_HEADS
D_FF = 7168
N_EXPERTS = 8
TOP_K = 2
ALPHA = (2.0 * DEPTH) ** 0.25
LN_EPS = 1e-6

N_LAT = BATCH * SEQ
N_CTX = BATCH * CTX_LEN
N_TOK = N_LAT + N_CTX
LANES = 128
SUBLANES = 8
VMEM_LIMIT = 52 * 1024 * 1024

COL_XR, COL_RGG, COL_Q, COL_K, COL_V, COL_O = range(6)

SEQ_ALL = CTX_LEN + SEQ
CONV_PAD = 8
MOE_TILE = 512
MOE_TILES = (N_LAT * TOP_K) // MOE_TILE + N_EXPERTS


def _cparams(*sem):
    return pltpu.CompilerParams(dimension_semantics=sem, vmem_limit_bytes=VMEM_LIMIT)


def _layer_norm(x):
    mu = jnp.mean(x, axis=-1, keepdims=True)
    xc = x - mu
    var = jnp.mean(xc * xc, axis=-1, keepdims=True)
    return xc * lax.rsqrt(var + LN_EPS)


def _sigmoid(x):
    return 0.5 * jnp.tanh(0.5 * x) + 0.5


def _softplus(x):
    return jnp.maximum(x, 0.0) + jnp.log1p(jnp.exp(-jnp.abs(x)))


def _mod_row(i, tm):
    return jnp.where(i >= N_LAT // tm, BATCH, i // (SEQ // tm))


def _mod_spec(chunk, tm):
    return pl.BlockSpec((1, 1, D_MODEL), lambda i, *_: (_mod_row(i, tm), 0, chunk))


def _modvec_kernel(s_ref, w_ref, b_ref, o_ref):
    s = s_ref[...]
    s = s * _sigmoid(s)
    o_ref[...] = jnp.dot(s.astype(BF16), w_ref[...].astype(BF16), preferred_element_type=F32) + b_ref[...]


def _modvec(cond, w, b, layer):
    n = w.shape[-1]
    tn = 1024
    out = pl.pallas_call(
        _modvec_kernel,
        grid=(n // tn,),
        in_specs=[pl.BlockSpec((SUBLANES, D_MODEL), lambda j: (0, 0)),
                  pl.BlockSpec((None, D_MODEL, tn), lambda j: (layer, 0, j)),
                  pl.BlockSpec((1, tn), lambda j: (0, j))],
        out_specs=pl.BlockSpec((SUBLANES, tn), lambda j: (0, j)),
        out_shape=jax.ShapeDtypeStruct((SUBLANES, n), F32),
        compiler_params=_cparams("arbitrary"),
        name="modvec",
    )(cond, w, b.reshape(1, n))
    return out.reshape(SUBLANES, 1, n)


def _lnmod_kernel(x_ref, sh_ref, sc_ref, wg_ref, u_ref, g_ref):
    u = (_layer_norm(x_ref[...]) * (1.0 + sc_ref[0]) + sh_ref[0]).astype(BF16)
    u_ref[...] = u
    g_ref[...] = jnp.dot(u, wg_ref[...], preferred_element_type=F32)


def _lnmod(h, mod, wg):
    r = h.shape[0]
    tm = 512
    return pl.pallas_call(
        _lnmod_kernel,
        grid=(r // tm,),
        in_specs=[pl.BlockSpec((tm, D_MODEL), lambda i: (i, 0)),
                  _mod_spec(0, tm), _mod_spec(1, tm),
                  pl.BlockSpec((D_MODEL, LANES), lambda i: (0, 0))],
        out_specs=[pl.BlockSpec((tm, D_MODEL), lambda i: (i, 0)),
                   pl.BlockSpec((tm, LANES), lambda i: (i, 0))],
        out_shape=[jax.ShapeDtypeStruct((r, D_MODEL), BF16), jax.ShapeDtypeStruct((r, LANES), F32)],
        compiler_params=_cparams("arbitrary"),
        name="ln_mod",
    )(h, mod, mod, wg)


def _proj_kernel(u_ref, w_ref, o_ref, wb_ref):
    @pl.when(pl.program_id(1) == 0)
    def _():
        wb_ref[...] = w_ref[...].astype(BF16)

    o_ref[...] = jnp.dot(u_ref[...], wb_ref[...], preferred_element_type=F32)


def _proj(u, w, layer, n):
    r = u.shape[0]
    tm, tn = 1024, 1024
    return pl.pallas_call(
        _proj_kernel,
        grid=(n // tn, r // tm),
        in_specs=[pl.BlockSpec((tm, D_MODEL), lambda j, i: (i, 0)),
                  pl.BlockSpec((None, D_MODEL, tn), lambda j, i: (layer, 0, j))],
        out_specs=pl.BlockSpec((tm, tn), lambda j, i: (i, j)),
        out_shape=jax.ShapeDtypeStruct((r, n), F32),
        scratch_shapes=[pltpu.VMEM((D_MODEL, tn), BF16)],
        compiler_params=_cparams("arbitrary", "arbitrary"),
        name="proj_in",
    )(u, w)


def _rg_coeffs(z, xc, sp, d):
    rg = _sigmoid(z[:, (2 * d) * LANES:(2 * d + 1) * LANES])
    ig = _sigmoid(z[:, (2 * d + 1) * LANES:(2 * d + 2) * LANES])
    log_a = (-RG_C) * rg * sp[d:d + 1, :]
    a = jnp.exp(log_a)
    return a, jnp.sqrt(jnp.tanh(-log_a) * (1.0 + a * a)) * (ig * xc)


def _rg_carries(h0_ref, hl_ref, hf, pf, hb, pb):
    h0 = h0_ref[...]
    cf = [None] * SUBLANES
    cbk = [None] * SUBLANES
    carry = h0[0:1, :]
    for s in range(SUBLANES):
        cf[s] = carry
        carry = pf[s:s + 1, :] * carry + hf[s:s + 1, :]
    hl_f = carry
    carry = h0[1:2, :]
    for s in reversed(range(SUBLANES)):
        cbk[s] = carry
        carry = pb[s:s + 1, :] * carry + hb[s:s + 1, :]
    hl_ref[...] = jnp.concatenate([hl_f, carry], axis=0)
    return cf, cbk


def _rg_kernel(x_ref, g_ref, cw_ref, cb_ref, w_ref, bias_ref, lam_ref, h0_ref, y_ref, hl_ref,
               xs, a0, b0, a1, b1, *, seq, tc, seg, pitch):
    zeros8 = jnp.zeros((CONV_PAD, LANES), F32)
    xs[0:CONV_PAD, :] = zeros8
    xs[CONV_PAD + seq:CONV_PAD + seq + CONV_PAD, :] = zeros8
    xs[CONV_PAD:CONV_PAD + seq, :] = x_ref[...]

    cw = cw_ref[...]
    sp = _softplus(-lam_ref[...])
    dec = ((a0, b0), (a1, b1))
    for c in range(seq // tc):
        r0 = c * tc
        xc = cb_ref[...] + cw[0:1, :] * xs[r0 + CONV_PAD - 2:r0 + CONV_PAD - 2 + tc, :]
        for j in range(1, CONV_W):
            xc = xc + cw[j:j + 1, :] * xs[r0 + CONV_PAD - 2 + j:r0 + CONV_PAD - 2 + j + tc, :]
        z = jnp.dot(xc.astype(BF16), w_ref[...], preferred_element_type=F32) + bias_ref[...]
        for d in range(2):
            a, b = _rg_coeffs(z, xc, sp, d)
            a_ref, b_ref = dec[d]
            for p in range(max(tc // seg, 1)):
                n = min(tc, seg)
                t = r0 + p * n
                dst = (t // seg) * pitch + t % seg
                a_ref[dst:dst + n, :] = a[p * n:(p + 1) * n, :]
                b_ref[dst:dst + n, :] = b[p * n:(p + 1) * n, :]

    def scan_step(i, carry):
        hf, pf, hb, pb = carry
        tf = i
        tb = seg - 1 - i
        af = a0[pl.ds(tf, SUBLANES, stride=pitch), :]
        bf = b0[pl.ds(tf, SUBLANES, stride=pitch), :]
        ab = a1[pl.ds(tb, SUBLANES, stride=pitch), :]
        bb = b1[pl.ds(tb, SUBLANES, stride=pitch), :]
        hf = af * hf + bf
        pf = pf * af
        hb = ab * hb + bb
        pb = pb * ab
        b0[pl.ds(tf, SUBLANES, stride=pitch), :] = hf
        a0[pl.ds(tf, SUBLANES, stride=pitch), :] = pf
        b1[pl.ds(tb, SUBLANES, stride=pitch), :] = hb
        a1[pl.ds(tb, SUBLANES, stride=pitch), :] = pb
        return hf, pf, hb, pb

    z8 = jnp.zeros((SUBLANES, LANES), F32)
    o8 = jnp.ones((SUBLANES, LANES), F32)
    hf, pf, hb, pb = lax.fori_loop(0, seg, scan_step, (z8, o8, z8, o8), unroll=4)

    cf, cbk = _rg_carries(h0_ref, hl_ref, hf, pf, hb, pb)
    for s in range(SUBLANES):
        for c in range(max(seg // tc, 1)):
            n = min(tc, seg)
            src = s * pitch + c * n
            t = s * seg + c * n
            h = (b0[src:src + n, :] + a0[src:src + n, :] * cf[s]
                 + b1[src:src + n, :] + a1[src:src + n, :] * cbk[s])
            y_ref[t:t + n, :] = (h * jax.nn.gelu(g_ref[t:t + n, :])).astype(y_ref.dtype)


def _rg_grid_kernel(x_ref, g_ref, cw_ref, cb_ref, w_ref, bias_ref, lam_ref, h0_ref, y_ref, hl_ref, a0, b0, a1, b1):
    n_rows = x_ref.shape[0] // GRID_W
    tc = 4 * n_rows
    cw = cw_ref[...]
    sp = _softplus(-lam_ref[...])
    dec = ((a0, b0), (a1, b1))
    rid = lax.broadcasted_iota(jnp.int32, (n_rows, LANES), 0)

    def column(c, shift):
        if shift == 0:
            return x_ref[c * n_rows:(c + 1) * n_rows, :]
        if shift < 0:
            v = x_ref[c * n_rows - 1:(c + 1) * n_rows - 1, :]
            return jnp.where(rid == 0, 0.0, v)
        v = x_ref[c * n_rows + 1:(c + 1) * n_rows + 1, :]
        return jnp.where(rid == n_rows - 1, 0.0, v)

    def tap(c0, off):
        lo = c0 + off
        if lo >= 0 and lo + 4 <= GRID_W:
            return x_ref[lo * n_rows:(lo + 4) * n_rows, :]
        parts = []
        for c in range(lo, lo + 4):
            if c < 0:
                parts.append(column(c + GRID_W, -1))
            elif c >= GRID_W:
                parts.append(column(c - GRID_W, 1))
            else:
                parts.append(column(c, 0))
        return jnp.concatenate(parts, axis=0)

    for k in range(GRID_W // 4):
        c0 = 4 * k
        xc = cb_ref[...] + cw[0:1, :] * tap(c0, -2)
        for j in range(1, CONV_W):
            xc = xc + cw[j:j + 1, :] * tap(c0, j - 2)
        z = jnp.dot(xc.astype(BF16), w_ref[...], preferred_element_type=F32) + bias_ref[...]
        for d in range(2):
            a, b = _rg_coeffs(z, xc, sp, d)
            dec[d][0][c0 * n_rows:c0 * n_rows + tc, :] = a
            dec[d][1][c0 * n_rows:c0 * n_rows + tc, :] = b

    seg = n_rows * GRID_W // SUBLANES

    def scan_step(i, carry):
        hf, pf, hb, pb = carry
        ib = seg - 1 - i
        tf = (i % GRID_W) * n_rows + i // GRID_W
        tb = (ib % GRID_W) * n_rows + ib // GRID_W
        af = a0[pl.ds(tf, SUBLANES, stride=SUBLANES), :]
        bf = b0[pl.ds(tf, SUBLANES, stride=SUBLANES), :]
        ab = a1[pl.ds(tb, SUBLANES, stride=SUBLANES), :]
        bb = b1[pl.ds(tb, SUBLANES, stride=SUBLANES), :]
        hf = af * hf + bf
        pf = pf * af
        hb = ab * hb + bb
        pb = pb * ab
        b0[pl.ds(tf, SUBLANES, stride=SUBLANES), :] = hf
        a0[pl.ds(tf, SUBLANES, stride=SUBLANES), :] = pf
        b1[pl.ds(tb, SUBLANES, stride=SUBLANES), :] = hb
        a1[pl.ds(tb, SUBLANES, stride=SUBLANES), :] = pb
        return hf, pf, hb, pb

    z8 = jnp.zeros((SUBLANES, LANES), F32)
    o8 = jnp.ones((SUBLANES, LANES), F32)
    hf, pf, hb, pb = lax.fori_loop(0, seg, scan_step, (z8, o8, z8, o8), unroll=4)

    cf, cbk = _rg_carries(h0_ref, hl_ref, hf, pf, hb, pb)
    cf4 = jnp.concatenate([jnp.broadcast_to(c, (SUBLANES, LANES)) for c in cf] * 4, axis=0)
    cb4 = jnp.concatenate([jnp.broadcast_to(c, (SUBLANES, LANES)) for c in cbk] * 4, axis=0)
    for k in range(GRID_W // 4):
        j0 = k * tc
        h = (b0[j0:j0 + tc, :] + a0[j0:j0 + tc, :] * cf4 + b1[j0:j0 + tc, :] + a1[j0:j0 + tc, :] * cb4)
        y_ref[j0:j0 + tc, :] = (h * jax.nn.gelu(g_ref[j0:j0 + tc, :])).astype(y_ref.dtype)


def _rg_branch(p, h0, cw, cb, wcat, bcat, lam, *, seq, row_block0, colmajor):
    nblk = W_RG // LANES
    if colmajor:
        kern = _rg_grid_kernel
        scratch = [pltpu.VMEM((seq, LANES), F32)] * 4
    else:
        tc = min(256, seq)
        seg = seq // SUBLANES
        pitch = seg + SUBLANES if (seg // SUBLANES) % 2 == 0 else seg
        kern = functools.partial(_rg_kernel, seq=seq, tc=tc, seg=seg, pitch=pitch)
        scratch = ([pltpu.VMEM((seq + 2 * CONV_PAD, LANES), F32)]
                   + [pltpu.VMEM((SUBLANES * pitch, LANES), F32)] * 4)
    in_specs = [
        pl.BlockSpec((seq, LANES), lambda b, c: (row_block0 + b, COL_XR * nblk + c)),
        pl.BlockSpec((seq, LANES), lambda b, c: (row_block0 + b, COL_RGG * nblk + c)),
        pl.BlockSpec((CONV_W, LANES), lambda b, c: (0, c)),
        pl.BlockSpec((1, LANES), lambda b, c: (0, c)),
        pl.BlockSpec((None, LANES, 4 * LANES), lambda b, c: (c, 0, 0)),
        pl.BlockSpec((None, 1, 4 * LANES), lambda b, c: (c, 0, 0)),
        pl.BlockSpec((2, LANES), lambda b, c: (0, c)),
        pl.BlockSpec((None, 2, LANES), lambda b, c: (b, 0, c)),
    ]
    out_shape = [jax.ShapeDtypeStruct((BATCH * seq, W_RG), BF16), jax.ShapeDtypeStruct((BATCH, 2, W_RG), F32)]
    out_specs = [pl.BlockSpec((seq, LANES), lambda b, c: (b, c)),
                 pl.BlockSpec((None, 2, LANES), lambda b, c: (b, 0, c))]
    return pl.pallas_call(
        kern,
        grid=(BATCH, nblk),
        in_specs=in_specs,
        out_specs=out_specs,
        out_shape=out_shape,
        scratch_shapes=scratch,
        compiler_params=_cparams("arbitrary", "arbitrary"),
        name=f"rglru_{seq}",
    )(p, p, cw, cb, wcat, bcat, lam, h0)


def _qkconv_kernel(lat_ref, ctx_ref, cw_ref, cb_ref, sc_ref, o_ref, xs, *, tc):
    cw = cw_ref[...]
    zeros = jnp.zeros((CONV_PAD, xs.shape[1]), F32)
    xs[0:CONV_PAD, :] = zeros
    for src_ref, out_off, n in ((ctx_ref, 0, CTX_LEN), (lat_ref, CTX_LEN, SEQ)):
        xs[CONV_PAD:CONV_PAD + n, :] = src_ref[...]
        xs[CONV_PAD + n:CONV_PAD + n + CONV_PAD, :] = zeros
        for c in range(n // tc):
            r0 = CONV_PAD + c * tc - 2
            xc = cb_ref[...] + cw[0:1, :] * xs[r0:r0 + tc, :]
            for j in range(1, CONV_W):
                xc = xc + cw[j:j + 1, :] * xs[r0 + j:r0 + j + tc, :]
            y = xc * _sigmoid(xc) * sc_ref[...]
            o_ref[out_off + c * tc:out_off + (c + 1) * tc, :] = y.astype(o_ref.dtype)


def _qkconv(p, cw, cb, scale):
    cwid = 256
    n = 2 * W_M
    col0 = COL_Q * D_MODEL // cwid
    return pl.pallas_call(
        functools.partial(_qkconv_kernel, tc=256),
        grid=(BATCH, n // cwid),
        in_specs=[pl.BlockSpec((SEQ, cwid), lambda b, c: (b, col0 + c)),
                  pl.BlockSpec((CTX_LEN, cwid), lambda b, c: (N_LAT // CTX_LEN + b, col0 + c)),
                  pl.BlockSpec((CONV_W, cwid), lambda b, c: (0, c)),
                  pl.BlockSpec((1, cwid), lambda b, c: (0, c)),
                  pl.BlockSpec((1, cwid), lambda b, c: (0, c))],
        out_specs=pl.BlockSpec((None, SEQ_ALL, cwid), lambda b, c: (b, 0, c)),
        out_shape=jax.ShapeDtypeStruct((BATCH, SEQ_ALL, n), BF16),
        scratch_shapes=[pltpu.VMEM((SEQ + 2 * CONV_PAD, cwid), F32)],
        compiler_params=_cparams("arbitrary", "arbitrary"),
        name="qk_conv",
    )(p, p, cw, cb, scale)


def _lane_cumsum(x, reverse):
    lane = lax.broadcasted_iota(jnp.int32, x.shape, 1)
    s = 1
    while s < LANES:
        if reverse:
            x = x + jnp.where(lane < LANES - s, pltpu.roll(x, LANES - s, axis=1), 0.0)
        else:
            x = x + jnp.where(lane >= s, pltpu.roll(x, s, axis=1), 0.0)
        s *= 2
    return x


def _mlstm_direction(q_ref, k_ref, v_ref, g, o_ref, c_ref, n_ref, m_ref, d):
    reverse = d == 1
    nt = (((1,), (1,)), ((), ()))
    gt = g.T
    base = d * 2 * M_HEADS
    i_rows = gt[base:base + M_HEADS, :]
    f_rows = -_softplus(-gt[base + M_HEADS:base + 2 * M_HEADS, :])
    bcum = _lane_cumsum(f_rows, reverse)
    r_rows = i_rows - bcum
    pad = jnp.zeros((M_CHUNK - M_HEADS, M_CHUNK), F32)
    r_cols = jnp.concatenate([r_rows, pad], axis=0).T
    last = 0 if reverse else M_CHUNK - 1

    src_i = lax.broadcasted_iota(jnp.int32, (M_CHUNK, M_CHUNK), 0)
    dst_i = lax.broadcasted_iota(jnp.int32, (M_CHUNK, M_CHUNK), 1)
    mask = (src_i >= dst_i) if reverse else (src_i <= dst_i)

    for h in range(M_HEADS):
        sl = slice(h * M_HEAD_DIM, (h + 1) * M_HEAD_DIM)
        qh = q_ref[:, sl]
        kh = k_ref[:, sl]
        vt = v_ref[:, sl].T
        b_row = bcum[h:h + 1, :]
        m_prev = m_ref[d, h:h + 1, :]
        b_last = jnp.broadcast_to(b_row[:, last:last + 1], (1, M_CHUNK))

        log_d = jnp.where(mask, b_row + r_cols[:, h:h + 1], -jnp.inf)
        log_inter = b_row + m_prev
        m_t = jnp.maximum(jnp.max(log_d, axis=0, keepdims=True), log_inter)
        st = lax.dot_general(kh, qh, nt, preferred_element_type=F32)
        st = st * jnp.exp(log_d - m_t)
        w_inter = jnp.exp(log_inter - m_t)
        c_prev = c_ref[d, h]
        n_prev = n_ref[d, h]
        cq = lax.dot_general(c_prev.astype(BF16), qh, nt, preferred_element_type=F32)
        num = jnp.dot(vt.astype(BF16), st.astype(BF16), preferred_element_type=F32) + w_inter * cq
        qn = lax.dot_general(n_prev.astype(BF16), qh, nt, preferred_element_type=F32)[0:1, :]
        den = jnp.sum(st, axis=0, keepdims=True) + w_inter * qn
        o_ref[:, sl] = (num / jnp.maximum(jnp.abs(den), jnp.exp(-m_t))).T

        log_w = b_last - b_row + i_rows[h:h + 1, :]
        m_new = jnp.maximum(b_last + m_prev, jnp.max(log_w, axis=1, keepdims=True))
        w = jnp.exp(log_w - m_new)
        decay = jnp.exp(b_last + m_prev - m_new)
        vw = (vt * w).astype(BF16)
        c_ref[d, h] = decay[:, 0:1] * c_prev + jnp.dot(vw, kh, preferred_element_type=F32)
        w8 = jnp.broadcast_to(w, (SUBLANES, M_CHUNK)).astype(BF16)
        n_ref[d, h] = decay[:, 0:1] * n_prev + jnp.dot(w8, kh, preferred_element_type=F32)
        m_ref[d, h:h + 1, :] = m_new


def _mlstm_kernel(qf_ref, kf_ref, vf_ref, gf_ref, qb_ref, kb_ref, vb_ref, gb_ref, bias_ref, of_ref, ob_ref,
                  c_ref, n_ref, m_ref):
    @pl.when(pl.program_id(1) == 0)
    def _():
        c_ref[...] = jnp.zeros(c_ref.shape, F32)
        n_ref[...] = jnp.zeros(n_ref.shape, F32)
        m_ref[...] = jnp.zeros(m_ref.shape, F32)

    bias = bias_ref[...]
    _mlstm_direction(qf_ref, kf_ref, vf_ref, gf_ref[...] + bias, of_ref, c_ref, n_ref, m_ref, 0)
    _mlstm_direction(qb_ref, kb_ref, vb_ref, gb_ref[...] + bias, ob_ref, c_ref, n_ref, m_ref, 1)


def _chunk_order(i, reverse):
    if not reverse:
        return i
    nctx = CTX_LEN // M_CHUNK
    ntot = SEQ_ALL // M_CHUNK
    return jnp.where(i < nctx, nctx - 1 - i, ntot + nctx - 1 - i)


def _token_chunk(b, i, reverse):
    nctx = CTX_LEN // M_CHUNK
    ci = _chunk_order(i, reverse)
    return jnp.where(ci < nctx, N_LAT // M_CHUNK + b * nctx + ci, b * (SEQ // M_CHUNK) + ci - nctx)


def _mlstm(q_and_k, p, g, gate_b):
    def blk(col, reverse):
        return pl.BlockSpec((None, M_CHUNK, W_M), lambda b, i: (b, _chunk_order(i, reverse), col))

    def tok(col, reverse, width=W_M):
        return pl.BlockSpec((M_CHUNK, width), lambda b, i: (_token_chunk(b, i, reverse), col))

    in_specs = [blk(0, False), blk(1, False), tok(COL_V, False), tok(0, False, LANES),
                blk(0, True), blk(1, True), tok(COL_V, True), tok(0, True, LANES),
                pl.BlockSpec((1, LANES), lambda b, i: (0, 0))]
    out = jax.ShapeDtypeStruct((N_TOK, W_M), F32)
    return pl.pallas_call(
        _mlstm_kernel,
        grid=(BATCH, SEQ_ALL // M_CHUNK),
        in_specs=in_specs,
        out_specs=[tok(0, False), tok(0, True)],
        out_shape=[out, out],
        scratch_shapes=[pltpu.VMEM((2, M_HEADS, M_HEAD_DIM, M_HEAD_DIM), F32),
                        pltpu.VMEM((2, M_HEADS, SUBLANES, M_HEAD_DIM), F32),
                        pltpu.VMEM((2, M_HEADS, LANES), F32)],
        compiler_params=_cparams("arbitrary", "arbitrary"),
        name="mlstm",
    )(q_and_k, q_and_k, p, g, q_and_k, q_and_k, p, g, gate_b)


def _merge_kernel(yr_ref, hf_ref, hb_ref, o_ref, ga_ref, gb_ref, gn_ref, prg_ref, pm_ref, mix_ref):
    hs = hf_ref[...] + hb_ref[...]
    hn = jnp.concatenate([_layer_norm(hs[:, h * M_HEAD_DIM:(h + 1) * M_HEAD_DIM]) for h in range(M_HEADS)], axis=1)
    ym = (hn * gn_ref[...] * _sigmoid(o_ref[...])).astype(BF16)
    a = jnp.dot(yr_ref[...], prg_ref[...], preferred_element_type=F32)
    b = jnp.dot(ym, pm_ref[...], preferred_element_type=F32)
    mix_ref[...] = (_sigmoid(ga_ref[...]) * a + _sigmoid(gb_ref[...]) * b).astype(BF16)


def _merge(yr, hf, hb, p1, p2, gn, p_rg, p_m, rows):
    tm = 256
    once = pl.Buffered(1)
    return pl.pallas_call(
        _merge_kernel,
        grid=(rows // tm,),
        in_specs=[pl.BlockSpec((tm, W_RG), lambda i: (i, 0)),
                  pl.BlockSpec((tm, W_M), lambda i: (i, 0)),
                  pl.BlockSpec((tm, W_M), lambda i: (i, 0)),
                  pl.BlockSpec((tm, W_M), lambda i: (i, COL_O)),
                  pl.BlockSpec((tm, D_MODEL), lambda i: (i, 0)),
                  pl.BlockSpec((tm, D_MODEL), lambda i: (i, 1)),
                  pl.BlockSpec((1, W_M), lambda i: (0, 0)),
                  pl.BlockSpec((W_RG, D_MODEL), lambda i: (0, 0), pipeline_mode=once),
                  pl.BlockSpec((W_M, D_MODEL), lambda i: (0, 0), pipeline_mode=once)],
        out_specs=pl.BlockSpec((tm, D_MODEL), lambda i: (i, 0)),
        out_shape=jax.ShapeDtypeStruct((rows, D_MODEL), BF16),
        compiler_params=_cparams("arbitrary"),
        name="merge",
    )(yr, hf, hb, p1, p2, p2, gn, p_rg, p_m)


def _res_ln(h, f, gate, lng, lnb):
    y = _layer_norm(ALPHA * h + gate * f)
    return y * lng + lnb


def _outproj_kernel(a_ref, w_ref, h_ref, g_ref, lng_ref, lnb_ref, o_ref):
    f = jnp.dot(a_ref[...], w_ref[...], preferred_element_type=F32)
    o_ref[...] = _res_ln(h_ref[...], f, g_ref[0], lng_ref[...], lnb_ref[...])


def _outproj(a, w, h, mod, gate_chunk, lng, lnb, rows):
    tm = 256
    return pl.pallas_call(
        _outproj_kernel,
        grid=(rows // tm,),
        in_specs=[pl.BlockSpec((tm, D_MODEL), lambda i: (i, 0)),
                  pl.BlockSpec((D_MODEL, D_MODEL), lambda i: (0, 0)),
                  pl.BlockSpec((tm, D_MODEL), lambda i: (i, 0)),
                  _mod_spec(gate_chunk, tm),
                  pl.BlockSpec((1, D_MODEL), lambda i: (0, 0)),
                  pl.BlockSpec((1, D_MODEL), lambda i: (0, 0))],
        out_specs=pl.BlockSpec((tm, D_MODEL), lambda i: (i, 0)),
        out_shape=jax.ShapeDtypeStruct((rows, D_MODEL), F32),
        compiler_params=_cparams("arbitrary"),
        name="out_proj",
    )(a, w, h, mod, lng, lnb)


def _ffn_kernel(h_ref, sh_ref, sc_ref, g_ref, w1_ref, w3_ref, w2_ref, lng_ref, lnb_ref, o_ref, u_ref, acc_ref):
    f = pl.program_id(1)

    @pl.when(f == 0)
    def _():
        u = _layer_norm(h_ref[...]) * (1.0 + sc_ref[0]) + sh_ref[0]
        u_ref[...] = u.astype(BF16)
        acc_ref[...] = jnp.zeros(acc_ref.shape, F32)

    u = u_ref[...]
    h1 = jnp.dot(u, w1_ref[...], preferred_element_type=F32)
    h3 = jnp.dot(u, w3_ref[...], preferred_element_type=F32)
    act = (h1 * _sigmoid(h1) * h3).astype(BF16)
    acc_ref[...] += jnp.dot(act, w2_ref[...], preferred_element_type=F32)

    @pl.when(f == pl.num_programs(1) - 1)
    def _():
        o_ref[...] = _res_ln(h_ref[...], acc_ref[...], g_ref[0], lng_ref[...], lnb_ref[...])


def _ffn(h, mod, w1, w3, w2, lng, lnb):
    rows = h.shape[0]
    tm, tf = 512, 512
    return pl.pallas_call(
        _ffn_kernel,
        grid=(rows // tm, D_FF // tf),
        in_specs=[pl.BlockSpec((tm, D_MODEL), lambda i, f: (i, 0)),
                  _mod_spec(3, tm), _mod_spec(4, tm), _mod_spec(5, tm),
                  pl.BlockSpec((D_MODEL, tf), lambda i, f: (0, f)),
                  pl.BlockSpec((D_MODEL, tf), lambda i, f: (0, f)),
                  pl.BlockSpec((tf, D_MODEL), lambda i, f: (f, 0)),
                  pl.BlockSpec((1, D_MODEL), lambda i, f: (0, 0)),
                  pl.BlockSpec((1, D_MODEL), lambda i, f: (0, 0))],
        out_specs=pl.BlockSpec((tm, D_MODEL), lambda i, f: (i, 0)),
        out_shape=jax.ShapeDtypeStruct((rows, D_MODEL), F32),
        scratch_shapes=[pltpu.VMEM((tm, D_MODEL), BF16), pltpu.VMEM((tm, D_MODEL), F32)],
        compiler_params=_cparams("arbitrary", "arbitrary"),
        name="ffn_dense",
    )(h, mod, mod, mod, w1, w3, w2, lng, lnb)


def _route_kernel(h_ref, sh_ref, sc_ref, rw_ref, rb_ref, t_ref, idx_ref, gate_ref):
    u = _layer_norm(h_ref[...]) * (1.0 + sc_ref[0]) + sh_ref[0]
    u = u.astype(BF16)
    t_ref[...] = u.astype(F32)
    lane = lax.broadcasted_iota(jnp.int32, (h_ref.shape[0], LANES), 1)
    logits = jnp.dot(u, rw_ref[...], preferred_element_type=F32) + rb_ref[...]
    logits = jnp.where(lane < N_EXPERTS, logits, -jnp.inf)
    v1 = jnp.max(logits, axis=-1, keepdims=True)
    i1 = jnp.min(jnp.where(logits == v1, lane, LANES), axis=-1, keepdims=True)
    rest = jnp.where(lane == i1, -jnp.inf, logits)
    v2 = jnp.max(rest, axis=-1, keepdims=True)
    i2 = jnp.min(jnp.where(rest == v2, lane, LANES), axis=-1, keepdims=True)
    e2 = jnp.exp(v2 - v1)
    g1 = 1.0 / (1.0 + e2)
    idx_ref[...] = jnp.where(lane == 0, i1, jnp.where(lane == 1, i2, 0))
    gate_ref[...] = jnp.where(lane == 0, g1, jnp.where(lane == 1, e2 * g1, 0.0))


def _route(h, mod, rw, rb):
    rows = h.shape[0]
    tm = 512
    return pl.pallas_call(
        _route_kernel,
        grid=(rows // tm,),
        in_specs=[pl.BlockSpec((tm, D_MODEL), lambda i: (i, 0)),
                  _mod_spec(3, tm), _mod_spec(4, tm),
                  pl.BlockSpec((D_MODEL, LANES), lambda i: (0, 0)),
                  pl.BlockSpec((1, LANES), lambda i: (0, 0))],
        out_specs=[pl.BlockSpec((tm, D_MODEL), lambda i: (i, 0)),
                   pl.BlockSpec((tm, LANES), lambda i: (i, 0)),
                   pl.BlockSpec((tm, LANES), lambda i: (i, 0))],
        out_shape=[jax.ShapeDtypeStruct((rows, D_MODEL), F32), jax.ShapeDtypeStruct((rows, LANES), jnp.int32),
                   jax.ShapeDtypeStruct((rows, LANES), F32)],
        compiler_params=_cparams("arbitrary"),
        name="moe_route",
    )(h, mod, mod, rw, rb)


def _gather_start(src_hbm, idx_ref, idx0, dst_ref, n, sem):
    def issue(r, carry):
        row = idx_ref[0, 0, idx0 + r]
        pltpu.make_async_copy(src_hbm.at[pl.ds(row, 1)], dst_ref.at[pl.ds(r, 1)], sem).start()
        return carry

    lax.fori_loop(0, n, issue, 0)


def _gather_wait(src_hbm, dst_ref, n, sem):
    pltpu.make_async_copy(src_hbm.at[pl.ds(0, n)], dst_ref, sem).wait()


def _experts_kernel(te_ref, tv_ref, tok_ref, nxt_ref, t_hbm, w1_ref, w3_ref, w2_ref, o_ref,
                    xf_ref, x_ref, acc_ref, sem):
    i = pl.program_id(0)
    f = pl.program_id(1)
    last_tile = pl.num_programs(0) - 1
    live = tv_ref[i] > 0
    next_live = jnp.logical_and(i < last_tile, tv_ref[jnp.minimum(i + 1, last_tile)] > 0)
    slot = i % 2

    @pl.when(jnp.logical_and(live, jnp.logical_and(f == 0, i == 0)))
    def _():
        _gather_start(t_hbm, tok_ref, 0, xf_ref.at[0], MOE_TILE, sem.at[0])

    @pl.when(jnp.logical_and(live, f == 0))
    def _():
        _gather_wait(t_hbm, xf_ref.at[slot], MOE_TILE, sem.at[slot])
        x_ref[...] = xf_ref[slot].astype(BF16)
        acc_ref[...] = jnp.zeros(acc_ref.shape, F32)

    @pl.when(jnp.logical_and(next_live, f == 0))
    def _():
        _gather_start(t_hbm, nxt_ref, 0, xf_ref.at[1 - slot], MOE_TILE, sem.at[1 - slot])

    @pl.when(live)
    def _():
        x = x_ref[...]
        h1 = jnp.dot(x, w1_ref[...], preferred_element_type=F32)
        h3 = jnp.dot(x, w3_ref[...], preferred_element_type=F32)
        act = (h1 * _sigmoid(h1) * h3).astype(BF16)
        acc_ref[...] += jnp.dot(act, w2_ref[...], preferred_element_type=F32)

    @pl.when(f == pl.num_programs(1) - 1)
    def _():
        o_ref[...] = jnp.where(live, acc_ref[...], 0.0)


def _experts(tile_expert, tile_live, row_token, t, w1, w3, w2):
    tf = 512
    grid_spec = pltpu.PrefetchScalarGridSpec(
        num_scalar_prefetch=2,
        grid=(MOE_TILES, D_FF // tf),
        in_specs=[pl.BlockSpec((1, 1, MOE_TILE), lambda i, f, te, tv: (i, 0, 0), memory_space=pltpu.SMEM),
                  pl.BlockSpec((1, 1, MOE_TILE), lambda i, f, te, tv: (jnp.minimum(i + 1, MOE_TILES - 1), 0, 0),
                               memory_space=pltpu.SMEM),
                  pl.BlockSpec(memory_space=pl.ANY),
                  pl.BlockSpec((None, D_MODEL, tf), lambda i, f, te, tv: (te[i], 0, f)),
                  pl.BlockSpec((None, D_MODEL, tf), lambda i, f, te, tv: (te[i], 0, f)),
                  pl.BlockSpec((None, tf, D_MODEL), lambda i, f, te, tv: (te[i], f, 0))],
        out_specs=pl.BlockSpec((MOE_TILE, D_MODEL), lambda i, f, te, tv: (i, 0)),
        scratch_shapes=[pltpu.VMEM((2, MOE_TILE, D_MODEL), F32), pltpu.VMEM((MOE_TILE, D_MODEL), BF16),
                        pltpu.VMEM((MOE_TILE, D_MODEL), F32), pltpu.SemaphoreType.DMA((2,))],
    )
    return pl.pallas_call(
        _experts_kernel,
        grid_spec=grid_spec,
        out_shape=jax.ShapeDtypeStruct((MOE_TILES * MOE_TILE, D_MODEL), F32),
        compiler_params=_cparams("arbitrary", "arbitrary"),
        name="moe_experts",
    )(tile_expert, tile_live, row_token, row_token, t, w1, w3, w2)


def _combine_kernel(pos_ref, nxt_ref, y_hbm, g_ref, h_ref, gm_ref, lng_ref, lnb_ref, o_ref, buf_ref, sem):
    i = pl.program_id(0)
    tm = h_ref.shape[0]
    slot = i % 2

    @pl.when(i == 0)
    def _():
        _gather_start(y_hbm, pos_ref, 0, buf_ref.at[0], 2 * tm, sem.at[0])

    @pl.when(i < pl.num_programs(0) - 1)
    def _():
        _gather_start(y_hbm, nxt_ref, 0, buf_ref.at[1 - slot], 2 * tm, sem.at[1 - slot])

    _gather_wait(y_hbm, buf_ref.at[slot], 2 * tm, sem.at[slot])
    g = g_ref[...]
    f = g[:, 0:1] * buf_ref[slot, 0:tm, :] + g[:, 1:2] * buf_ref[slot, tm:2 * tm, :]
    o_ref[...] = _res_ln(h_ref[...], f, gm_ref[0], lng_ref[...], lnb_ref[...])


def _combine(pos, yb, gates, h, mod, lng, lnb):
    rows = h.shape[0]
    tm = pos.shape[-1] // 2
    nt = rows // tm
    return pl.pallas_call(
        _combine_kernel,
        grid=(nt,),
        in_specs=[pl.BlockSpec((1, 1, 2 * tm), lambda i: (i, 0, 0), memory_space=pltpu.SMEM),
                  pl.BlockSpec((1, 1, 2 * tm), lambda i: (jnp.minimum(i + 1, nt - 1), 0, 0), memory_space=pltpu.SMEM),
                  pl.BlockSpec(memory_space=pl.ANY),
                  pl.BlockSpec((tm, LANES), lambda i: (i, 0)),
                  pl.BlockSpec((tm, D_MODEL), lambda i: (i, 0)),
                  _mod_spec(5, tm),
                  pl.BlockSpec((1, D_MODEL), lambda i: (0, 0)),
                  pl.BlockSpec((1, D_MODEL), lambda i: (0, 0))],
        out_specs=pl.BlockSpec((tm, D_MODEL), lambda i: (i, 0)),
        out_shape=jax.ShapeDtypeStruct((rows, D_MODEL), F32),
        scratch_shapes=[pltpu.VMEM((2, 2 * tm, D_MODEL), F32), pltpu.SemaphoreType.DMA((2,))],
        compiler_params=_cparams("arbitrary"),
        name="moe_combine",
    )(pos, pos, yb, gates, h, mod, lng, lnb)


def _cast_kernel(w_ref, o_ref):
    o_ref[...] = w_ref[...].astype(BF16)


def _to_bf16(w, tr):
    r, c = w.shape
    return pl.pallas_call(
        _cast_kernel,
        grid=(r // tr,),
        in_specs=[pl.BlockSpec((tr, c), lambda i: (i, 0))],
        out_specs=pl.BlockSpec((tr, c), lambda i: (i, 0)),
        out_shape=jax.ShapeDtypeStruct((r, c), BF16),
        compiler_params=_cparams("arbitrary"),
        name="cast_bf16",
    )(w)


def _moe(h, mod, router_w, router_b, w1, w3, w2, lng, lnb):
    n = h.shape[0]
    rw = jnp.zeros((D_MODEL, LANES), BF16).at[:, :N_EXPERTS].set(router_w.astype(BF16))
    rb = jnp.zeros((1, LANES), F32).at[0, :N_EXPERTS].set(router_b)
    t, idx, gates = _route(h, mod, rw, rb)
    expert = idx[:, :TOP_K].reshape(-1)
    onehot = (expert[:, None] == jnp.arange(N_EXPERTS)[None, :]).astype(jnp.int32)
    csum = jnp.cumsum(onehot, axis=0)
    rank = jnp.sum(csum * onehot, axis=1) - 1
    counts = csum[-1]
    padded = (counts + MOE_TILE - 1) // MOE_TILE * MOE_TILE
    pend = jnp.cumsum(padded)
    pos = (pend - padded)[expert] + rank
    n_rows = MOE_TILES * MOE_TILE
    row_token = jnp.zeros((n_rows,), jnp.int32).at[pos].set(jnp.arange(n * TOP_K, dtype=jnp.int32) // TOP_K)
    tile_start = jnp.arange(MOE_TILES) * MOE_TILE
    tile_expert = jnp.minimum(jnp.searchsorted(pend, tile_start, side='right'), N_EXPERTS - 1).astype(jnp.int32)
    tile_live = (tile_start < pend[-1]).astype(jnp.int32)
    yb = _experts(tile_expert, tile_live, row_token.reshape(MOE_TILES, 1, MOE_TILE), t, w1, w3, w2)
    tm = 256
    pos2 = pos.reshape(n // tm, tm, TOP_K).transpose(0, 2, 1).reshape(n // tm, 1, TOP_K * tm).astype(jnp.int32)
    return _combine(pos2, yb, gates, h, mod, lng, lnb)


def _swap_grid_order(t):
    b, l, c = t.shape
    return t.reshape(b, l // GRID_W, GRID_W, c).transpose(0, 2, 1, 3).reshape(b, l, c)


def _mixer(h, mod, w_in_all, layer, conv_rg_w, conv_rg_b, conv_m_w, conv_m_b, rg_wa, rg_ba, rg_wx, rg_bx, rg_lam,
           m_gate_b, m_gn_g, p_rg, p_m, rows_out):
    gate0 = C_M + 4 * W_M
    w_tail = w_in_all[layer, :, gate0:]
    w_gate = jnp.zeros((D_MODEL, LANES), BF16).at[:, :N_GATES].set(w_tail[:, :N_GATES].astype(BF16))
    u, pg = _lnmod(h, mod, w_gate)
    p = _proj(u, w_in_all[layer:layer + 1, :, :gate0], 0, gate0)
    p2 = _proj(u, w_tail[None, :, N_GATES:], 0, 2 * D_MODEL)

    wcat = jnp.concatenate([rg_wa[0], rg_wx[0], rg_wa[1], rg_wx[1]], axis=-1).astype(BF16)
    bcat = jnp.stack([rg_ba[0], rg_bx[0], rg_ba[1], rg_bx[1]], axis=0)
    bcat = bcat.reshape(4, RG_BLOCKS, RG_BLOCK).transpose(1, 0, 2).reshape(RG_BLOCKS, 1, 4 * RG_BLOCK)
    cb = conv_rg_b.reshape(1, W_RG)
    h0 = jnp.zeros((BATCH, 2, W_RG), F32)
    yr_c, st = _rg_branch(p, h0, conv_rg_w, cb, wcat, bcat, rg_lam, seq=CTX_LEN, row_block0=N_LAT // CTX_LEN,
                          colmajor=False)
    yr, _ = _rg_branch(p, st, conv_rg_w, cb, wcat, bcat, rg_lam, seq=SEQ, row_block0=0, colmajor=True)
    if rows_out > N_LAT:
        yr = jnp.concatenate([yr, yr_c], axis=0)

    qscale = jnp.concatenate([jnp.full((1, W_M), M_HEAD_DIM ** -0.5, F32), jnp.ones((1, W_M), F32)], axis=1)
    qk = _qkconv(p, conv_m_w, conv_m_b.reshape(1, 2 * W_M), qscale)
    gate_b = jnp.zeros((1, LANES), F32).at[0, :N_GATES].set(m_gate_b.reshape(-1))
    h_f, h_b = _mlstm(qk, p, pg, gate_b)

    return _merge(yr, h_f, h_b, p, p2, m_gn_g.reshape(1, W_M), p_rg.astype(BF16), p_m.astype(BF16), rows_out)


def kernel(x, c, ctx, c_ctx, w_mod, b_mod, w_in, conv_rg_w, conv_rg_b, conv_m_w, conv_m_b, rg_wa, rg_ba, rg_wx, rg_bx, rg_lam, m_gate_b, m_gn_g, p_rg, p_m, w_out, ln_g, ln_b, ff_w1, ff_w3, ff_w2, router_w, router_b, ex_w1, ex_w3, ex_w2):
    h = jnp.concatenate([_swap_grid_order(x).reshape(N_LAT, D_MODEL), ctx.reshape(N_CTX, D_MODEL)], axis=0)
    cond = jnp.zeros((SUBLANES, D_MODEL), F32).at[:BATCH].set(c).at[BATCH].set(c_ctx)
    for l in range(DEPTH):
        last = l == DEPTH - 1
        rows = N_LAT if last else N_TOK
        mod = _modvec(cond, w_mod, b_mod[l], l)
        mix = _mixer(h, mod, w_in, l, conv_rg_w[l], conv_rg_b[l], conv_m_w[l], conv_m_b[l], rg_wa[l], rg_ba[l],
                     rg_wx[l], rg_bx[l], rg_lam[l], m_gate_b[l], m_gn_g[l], p_rg[l], p_m[l], rows)
        lng = ln_g[l].reshape(2, 1, D_MODEL)
        lnb = ln_b[l].reshape(2, 1, D_MODEL)
        h = _outproj(mix, w_out[l].astype(BF16), h, mod, 2, lng[0], lnb[0], rows)
        k = l // 2
        if l % 2 == 0:
            h = _ffn(h, mod, _to_bf16(ff_w1[k], 256), _to_bf16(ff_w3[k], 256), _to_bf16(ff_w2[k], 1024),
                     lng[1], lnb[1])
        else:
            w1 = _to_bf16(ex_w1[k].reshape(N_EXPERTS * D_MODEL, D_FF), 256).reshape(N_EXPERTS, D_MODEL, D_FF)
            w3 = _to_bf16(ex_w3[k].reshape(N_EXPERTS * D_MODEL, D_FF), 256).reshape(N_EXPERTS, D_MODEL, D_FF)
            w2 = _to_bf16(ex_w2[k].reshape(N_EXPERTS * D_FF, D_MODEL), 1024).reshape(N_EXPERTS, D_FF, D_MODEL)
            h = _moe(h, mod, router_w[k], router_b[k], w1, w3, w2, lng[1], lnb[1])
    return _swap_grid_order(h[:N_LAT].reshape(BATCH, SEQ, D_MODEL))
```

```python
import functools

import jax
import jax.numpy as jnp
from jax import lax
from jax.experimental import pallas as pl
from jax.experimental.pallas import tpu as pltpu

F32 = jnp.float32
BF16 = jnp.bfloat16

D_MODEL = 2048
BATCH = 4
SEQ = 4096
DEPTH = 2
GRID_W = 64
CTX_LEN = 256
W_RG = 2048
RG_BLOCKS = 16
RG_BLOCK = W_RG // RG_BLOCKS
RG_C = 8.0
CONV_W = 4
M_HEADS = 8
M_HEAD_DIM = 256
W_M = M_HEADS * M_HEAD_DIM
M_CHUNK = 128
C_M = 2 * W_RG
N_GATES = 4 * M_HEADS
D_FF = 7168
N_EXPERTS = 8
TOP_K = 2
ALPHA = (2.0 * DEPTH) ** 0.25
LN_EPS = 1e-6

N_LAT = BATCH * SEQ
N_CTX = BATCH * CTX_LEN
N_TOK = N_LAT + N_CTX
LANES = 128
SUBLANES = 8
VMEM_LIMIT = 52 * 1024 * 1024

COL_XR, COL_RGG, COL_Q, COL_K, COL_V, COL_O = range(6)

SEQ_ALL = CTX_LEN + SEQ
CONV_PAD = 8
MOE_TILE = 512
MOE_TILES = (N_LAT * TOP_K) // MOE_TILE + N_EXPERTS


def _cparams(*sem):
    return pltpu.CompilerParams(dimension_semantics=sem, vmem_limit_bytes=VMEM_LIMIT)


def _layer_norm(x):
    mu = jnp.mean(x, axis=-1, keepdims=True)
    xc = x - mu
    var = jnp.mean(xc * xc, axis=-1, keepdims=True)
    return xc * lax.rsqrt(var + LN_EPS)


def _sigmoid(x):
    return 0.5 * jnp.tanh(0.5 * x) + 0.5


def _softplus(x):
    return jnp.maximum(x, 0.0) + jnp.log1p(jnp.exp(-jnp.abs(x)))


def _mod_row(i, tm):
    return jnp.where(i >= N_LAT // tm, BATCH, i // (SEQ // tm))


def _mod_spec(chunk, tm):
    return pl.BlockSpec((1, 1, D_MODEL), lambda i, *_: (_mod_row(i, tm), 0, chunk))


def _modvec_kernel(s_ref, w_ref, b_ref, o_ref):
    s = s_ref[...]
    s = s * _sigmoid(s)
    o_ref[...] = jnp.dot(s.astype(BF16), w_ref[...].astype(BF16), preferred_element_type=F32) + b_ref[...]


def _modvec(cond, w, b, layer):
    n = w.shape[-1]
    tn = 1024
    out = pl.pallas_call(
        _modvec_kernel,
        grid=(n // tn,),
        in_specs=[pl.BlockSpec((SUBLANES, D_MODEL), lambda j: (0, 0)),
                  pl.BlockSpec((None, D_MODEL, tn), lambda j: (layer, 0, j)),
                  pl.BlockSpec((1, tn), lambda j: (0, j))],
        out_specs=pl.BlockSpec((SUBLANES, tn), lambda j: (0, j)),
        out_shape=jax.ShapeDtypeStruct((SUBLANES, n), F32),
        compiler_params=_cparams("arbitrary"),
        name="modvec",
    )(cond, w, b.reshape(1, n))
    return out.reshape(SUBLANES, 1, n)


def _lnmod_kernel(x_ref, sh_ref, sc_ref, wg_ref, u_ref, g_ref):
    u = (_layer_norm(x_ref[...]) * (1.0 + sc_ref[0]) + sh_ref[0]).astype(BF16)
    u_ref[...] = u
    g_ref[...] = jnp.dot(u, wg_ref[...], preferred_element_type=F32)


def _lnmod(h, mod, wg):
    r = h.shape[0]
    tm = 512
    return pl.pallas_call(
        _lnmod_kernel,
        grid=(r // tm,),
        in_specs=[pl.BlockSpec((tm, D_MODEL), lambda i: (i, 0)),
                  _mod_spec(0, tm), _mod_spec(1, tm),
                  pl.BlockSpec((D_MODEL, LANES), lambda i: (0, 0))],
        out_specs=[pl.BlockSpec((tm, D_MODEL), lambda i: (i, 0)),
                   pl.BlockSpec((tm, LANES), lambda i: (i, 0))],
        out_shape=[jax.ShapeDtypeStruct((r, D_MODEL), BF16), jax.ShapeDtypeStruct((r, LANES), F32)],
        compiler_params=_cparams("arbitrary"),
        name="ln_mod",
    )(h, mod, mod, wg)


def _proj_kernel(u_ref, w_ref, o_ref, wb_ref):
    @pl.when(pl.program_id(1) == 0)
    def _():
        wb_ref[...] = w_ref[...].astype(BF16)

    o_ref[...] = jnp.dot(u_ref[...], wb_ref[...], preferred_element_type=F32)


def _proj(u, w, layer, n):
    r = u.shape[0]
    tm, tn = 1024, 1024
    return pl.pallas_call(
        _proj_kernel,
        grid=(n // tn, r // tm),
        in_specs=[pl.BlockSpec((tm, D_MODEL), lambda j, i: (i, 0)),
                  pl.BlockSpec((None, D_MODEL, tn), lambda j, i: (layer, 0, j))],
        out_specs=pl.BlockSpec((tm, tn), lambda j, i: (i, j)),
        out_shape=jax.ShapeDtypeStruct((r, n), F32),
        scratch_shapes=[pltpu.VMEM((D_MODEL, tn), BF16)],
        compiler_params=_cparams("arbitrary", "arbitrary"),
        name="proj_in",
    )(u, w)


def _rg_coeffs(zh, hx, sp4, d):
    tr = jnp.tanh(zh[:, (2 * d) * LANES:(2 * d + 1) * LANES])
    ti = jnp.tanh(zh[:, (2 * d + 1) * LANES:(2 * d + 2) * LANES])
    neg_log_a = sp4[d:d + 1, :] * (tr + 1.0)
    a = jnp.exp(-neg_log_a)
    return a, jnp.sqrt(jnp.tanh(neg_log_a) * (1.0 + a * a)) * ((ti + 1.0) * hx)


def _rg_carries(h0_ref, hl_ref, hf, pf, hb, pb):
    h0 = h0_ref[...]
    cf = [None] * SUBLANES
    cbk = [None] * SUBLANES
    carry = h0[0:1, :]
    for s in range(SUBLANES):
        cf[s] = carry
        carry = pf[s:s + 1, :] * carry + hf[s:s + 1, :]
    hl_f = carry
    carry = h0[1:2, :]
    for s in reversed(range(SUBLANES)):
        cbk[s] = carry
        carry = pb[s:s + 1, :] * carry + hb[s:s + 1, :]
    hl_ref[...] = jnp.concatenate([hl_f, carry], axis=0)
    return cf, cbk


def _rg_kernel(x_ref, g_ref, cw_ref, cb_ref, w_ref, bias_ref, lam_ref, h0_ref, y_ref, hl_ref,
               xs, a0, b0, a1, b1, *, seq, tc, seg, pitch):
    zeros8 = jnp.zeros((CONV_PAD, LANES), F32)
    xs[0:CONV_PAD, :] = zeros8
    xs[CONV_PAD + seq:CONV_PAD + seq + CONV_PAD, :] = zeros8
    xs[CONV_PAD:CONV_PAD + seq, :] = x_ref[...]

    cw = cw_ref[...]
    sp4 = (0.5 * RG_C) * _softplus(-lam_ref[...])
    dec = ((a0, b0), (a1, b1))
    for c in range(seq // tc):
        r0 = c * tc
        xc = cb_ref[...] + cw[0:1, :] * xs[r0 + CONV_PAD - 2:r0 + CONV_PAD - 2 + tc, :]
        for j in range(1, CONV_W):
            xc = xc + cw[j:j + 1, :] * xs[r0 + CONV_PAD - 2 + j:r0 + CONV_PAD - 2 + j + tc, :]
        zh = jnp.dot(xc.astype(BF16), w_ref[...], preferred_element_type=F32) + bias_ref[...]
        hx = 0.5 * xc
        for d in range(2):
            a, b = _rg_coeffs(zh, hx, sp4, d)
            a_ref, b_ref = dec[d]
            for p in range(max(tc // seg, 1)):
                n = min(tc, seg)
                t = r0 + p * n
                dst = (t // seg) * pitch + t % seg
                a_ref[dst:dst + n, :] = a[p * n:(p + 1) * n, :]
                b_ref[dst:dst + n, :] = b[p * n:(p + 1) * n, :]

    def scan_step(i, carry):
        hf, pf, hb, pb = carry
        tf = i
        tb = seg - 1 - i
        af = a0[pl.ds(tf, SUBLANES, stride=pitch), :]
        bf = b0[pl.ds(tf, SUBLANES, stride=pitch), :]
        ab = a1[pl.ds(tb, SUBLANES, stride=pitch), :]
        bb = b1[pl.ds(tb, SUBLANES, stride=pitch), :]
        hf = af * hf + bf
        pf = pf * af
        hb = ab * hb + bb
        pb = pb * ab
        b0[pl.ds(tf, SUBLANES, stride=pitch), :] = hf
        a0[pl.ds(tf, SUBLANES, stride=pitch), :] = pf
        b1[pl.ds(tb, SUBLANES, stride=pitch), :] = hb
        a1[pl.ds(tb, SUBLANES, stride=pitch), :] = pb
        return hf, pf, hb, pb

    z8 = jnp.zeros((SUBLANES, LANES), F32)
    o8 = jnp.ones((SUBLANES, LANES), F32)
    hf, pf, hb, pb = lax.fori_loop(0, seg, scan_step, (z8, o8, z8, o8), unroll=4)

    cf, cbk = _rg_carries(h0_ref, hl_ref, hf, pf, hb, pb)
    for s in range(SUBLANES):
        for c in range(max(seg // tc, 1)):
            n = min(tc, seg)
            src = s * pitch + c * n
            t = s * seg + c * n
            h = (b0[src:src + n, :] + a0[src:src + n, :] * cf[s]
                 + b1[src:src + n, :] + a1[src:src + n, :] * cbk[s])
            y_ref[t:t + n, :] = (h * jax.nn.gelu(g_ref[t:t + n, :])).astype(y_ref.dtype)


def _rg_grid_kernel(x_ref, g_ref, cw_ref, cb_ref, w_ref, bias_ref, lam_ref, h0_ref, y_ref, hl_ref, a0, b0, a1, b1):
    n_rows = x_ref.shape[0] // GRID_W
    tc = 4 * n_rows
    cw = cw_ref[...]
    sp4 = (0.5 * RG_C) * _softplus(-lam_ref[...])
    dec = ((a0, b0), (a1, b1))
    rid = lax.broadcasted_iota(jnp.int32, (n_rows, LANES), 0)

    def column(c, shift):
        if shift == 0:
            return x_ref[c * n_rows:(c + 1) * n_rows, :]
        if shift < 0:
            v = x_ref[c * n_rows - 1:(c + 1) * n_rows - 1, :]
            return jnp.where(rid == 0, 0.0, v)
        v = x_ref[c * n_rows + 1:(c + 1) * n_rows + 1, :]
        return jnp.where(rid == n_rows - 1, 0.0, v)

    def tap(c0, off):
        lo = c0 + off
        if lo >= 0 and lo + 4 <= GRID_W:
            return x_ref[lo * n_rows:(lo + 4) * n_rows, :]
        parts = []
        for c in range(lo, lo + 4):
            if c < 0:
                parts.append(column(c + GRID_W, -1))
            elif c >= GRID_W:
                parts.append(column(c - GRID_W, 1))
            else:
                parts.append(column(c, 0))
        return jnp.concatenate(parts, axis=0)

    for k in range(GRID_W // 4):
        c0 = 4 * k
        xc = cb_ref[...] + cw[0:1, :] * tap(c0, -2)
        for j in range(1, CONV_W):
            xc = xc + cw[j:j + 1, :] * tap(c0, j - 2)
        zh = jnp.dot(xc.astype(BF16), w_ref[...], preferred_element_type=F32) + bias_ref[...]
        hx = 0.5 * xc
        for d in range(2):
            a, b = _rg_coeffs(zh, hx, sp4, d)
            dec[d][0][c0 * n_rows:c0 * n_rows + tc, :] = a
            dec[d][1][c0 * n_rows:c0 * n_rows + tc, :] = b

    def scan_steps(rl):
        def scan_step(c, carry):
            hf, pf, hb, pb = carry
            tf = c * n_rows + rl
            tb = (GRID_W - 1 - c) * n_rows + (SUBLANES - 1 - rl)
            af = a0[pl.ds(tf, SUBLANES, stride=SUBLANES), :]
            bf = b0[pl.ds(tf, SUBLANES, stride=SUBLANES), :]
            ab = a1[pl.ds(tb, SUBLANES, stride=SUBLANES), :]
            bb = b1[pl.ds(tb, SUBLANES, stride=SUBLANES), :]
            hf = af * hf + bf
            pf = pf * af
            hb = ab * hb + bb
            pb = pb * ab
            b0[pl.ds(tf, SUBLANES, stride=SUBLANES), :] = hf
            a0[pl.ds(tf, SUBLANES, stride=SUBLANES), :] = pf
            b1[pl.ds(tb, SUBLANES, stride=SUBLANES), :] = hb
            a1[pl.ds(tb, SUBLANES, stride=SUBLANES), :] = pb
            return hf, pf, hb, pb
        return scan_step

    z8 = jnp.zeros((SUBLANES, LANES), F32)
    o8 = jnp.ones((SUBLANES, LANES), F32)
    carry = (z8, o8, z8, o8)
    for rl in range(SUBLANES):
        carry = lax.fori_loop(0, GRID_W, scan_steps(rl), carry, unroll=4)
    hf, pf, hb, pb = carry

    cf, cbk = _rg_carries(h0_ref, hl_ref, hf, pf, hb, pb)
    cf4 = jnp.concatenate([jnp.broadcast_to(c, (SUBLANES, LANES)) for c in cf] * 4, axis=0)
    cb4 = jnp.concatenate([jnp.broadcast_to(c, (SUBLANES, LANES)) for c in cbk] * 4, axis=0)
    for k in range(GRID_W // 4):
        j0 = k * tc
        h = (b0[j0:j0 + tc, :] + a0[j0:j0 + tc, :] * cf4 + b1[j0:j0 + tc, :] + a1[j0:j0 + tc, :] * cb4)
        y_ref[j0:j0 + tc, :] = (h * jax.nn.gelu(g_ref[j0:j0 + tc, :])).astype(y_ref.dtype)


def _rg_branch(p, h0, cw, cb, wcat, bcat, lam, *, seq, row_block0, colmajor):
    nblk = W_RG // LANES
    if colmajor:
        kern = _rg_grid_kernel
        scratch = [pltpu.VMEM((seq, LANES), F32)] * 4
    else:
        tc = min(256, seq)
        seg = seq // SUBLANES
        pitch = seg + SUBLANES if (seg // SUBLANES) % 2 == 0 else seg
        kern = functools.partial(_rg_kernel, seq=seq, tc=tc, seg=seg, pitch=pitch)
        scratch = ([pltpu.VMEM((seq + 2 * CONV_PAD, LANES), F32)]
                   + [pltpu.VMEM((SUBLANES * pitch, LANES), F32)] * 4)
    in_specs = [
        pl.BlockSpec((seq, LANES), lambda b, c: (row_block0 + b, COL_XR * nblk + c)),
        pl.BlockSpec((seq, LANES), lambda b, c: (row_block0 + b, COL_RGG * nblk + c)),
        pl.BlockSpec((CONV_W, LANES), lambda b, c: (0, c)),
        pl.BlockSpec((1, LANES), lambda b, c: (0, c)),
        pl.BlockSpec((None, LANES, 4 * LANES), lambda b, c: (c, 0, 0)),
        pl.BlockSpec((None, 1, 4 * LANES), lambda b, c: (c, 0, 0)),
        pl.BlockSpec((2, LANES), lambda b, c: (0, c)),
        pl.BlockSpec((None, 2, LANES), lambda b, c: (b, 0, c)),
    ]
    out_shape = [jax.ShapeDtypeStruct((BATCH * seq, W_RG), BF16), jax.ShapeDtypeStruct((BATCH, 2, W_RG), F32)]
    out_specs = [pl.BlockSpec((seq, LANES), lambda b, c: (b, c)),
                 pl.BlockSpec((None, 2, LANES), lambda b, c: (b, 0, c))]
    return pl.pallas_call(
        kern,
        grid=(BATCH, nblk),
        in_specs=in_specs,
        out_specs=out_specs,
        out_shape=out_shape,
        scratch_shapes=scratch,
        compiler_params=_cparams("arbitrary", "arbitrary"),
        name=f"rglru_{seq}",
    )(p, p, cw, cb, wcat, bcat, lam, h0)


def _qkconv_kernel(lat_ref, ctx_ref, cw_ref, cb_ref, sc_ref, o_ref, xs, *, tc):
    cw = cw_ref[...]
    zeros = jnp.zeros((CONV_PAD, xs.shape[1]), F32)
    xs[0:CONV_PAD, :] = zeros
    for src_ref, out_off, n in ((ctx_ref, 0, CTX_LEN), (lat_ref, CTX_LEN, SEQ)):
        xs[CONV_PAD:CONV_PAD + n, :] = src_ref[...]
        xs[CONV_PAD + n:CONV_PAD + n + CONV_PAD, :] = zeros
        for c in range(n // tc):
            r0 = CONV_PAD + c * tc - 2
            xc = cb_ref[...] + cw[0:1, :] * xs[r0:r0 + tc, :]
            for j in range(1, CONV_W):
                xc = xc + cw[j:j + 1, :] * xs[r0 + j:r0 + j + tc, :]
            y = xc * _sigmoid(xc) * sc_ref[...]
            o_ref[out_off + c * tc:out_off + (c + 1) * tc, :] = y.astype(o_ref.dtype)


def _qkconv(p, cw, cb, scale):
    cwid = 256
    n = 2 * W_M
    col0 = COL_Q * D_MODEL // cwid
    return pl.pallas_call(
        functools.partial(_qkconv_kernel, tc=256),
        grid=(BATCH, n // cwid),
        in_specs=[pl.BlockSpec((SEQ, cwid), lambda b, c: (b, col0 + c)),
                  pl.BlockSpec((CTX_LEN, cwid), lambda b, c: (N_LAT // CTX_LEN + b, col0 + c)),
                  pl.BlockSpec((CONV_W, cwid), lambda b, c: (0, c)),
                  pl.BlockSpec((1, cwid), lambda b, c: (0, c)),
                  pl.BlockSpec((1, cwid), lambda b, c: (0, c))],
        out_specs=pl.BlockSpec((None, SEQ_ALL, cwid), lambda b, c: (b, 0, c)),
        out_shape=jax.ShapeDtypeStruct((BATCH, SEQ_ALL, n), BF16),
        scratch_shapes=[pltpu.VMEM((SEQ + 2 * CONV_PAD, cwid), F32)],
        compiler_params=_cparams("arbitrary", "arbitrary"),
        name="qk_conv",
    )(p, p, cw, cb, scale)


def _lane_cumsum(x, reverse):
    lane = lax.broadcasted_iota(jnp.int32, x.shape, 1)
    s = 1
    while s < LANES:
        if reverse:
            x = x + jnp.where(lane < LANES - s, pltpu.roll(x, LANES - s, axis=1), 0.0)
        else:
            x = x + jnp.where(lane >= s, pltpu.roll(x, s, axis=1), 0.0)
        s *= 2
    return x


def _mlstm_direction(q_ref, k_ref, v_ref, g, o_ref, c_ref, n_ref, m_ref, d):
    reverse = d == 1
    nt = (((1,), (1,)), ((), ()))
    gt = g.T
    base = d * 2 * M_HEADS
    i_rows = gt[base:base + M_HEADS, :]
    f_rows = -_softplus(-gt[base + M_HEADS:base + 2 * M_HEADS, :])
    bcum = _lane_cumsum(f_rows, reverse)
    r_rows = i_rows - bcum
    pad = jnp.zeros((M_CHUNK - M_HEADS, M_CHUNK), F32)
    r_cols = jnp.concatenate([r_rows, pad], axis=0).T
    last = 0 if reverse else M_CHUNK - 1

    src_i = lax.broadcasted_iota(jnp.int32, (M_CHUNK, M_CHUNK), 0)
    dst_i = lax.broadcasted_iota(jnp.int32, (M_CHUNK, M_CHUNK), 1)
    mask = (src_i >= dst_i) if reverse else (src_i <= dst_i)

    for h in range(M_HEADS):
        sl = slice(h * M_HEAD_DIM, (h + 1) * M_HEAD_DIM)
        qh = q_ref[:, sl]
        kh = k_ref[:, sl]
        vt = v_ref[:, sl].T
        b_row = bcum[h:h + 1, :]
        m_prev = m_ref[d, h:h + 1, :]
        b_last = jnp.broadcast_to(b_row[:, last:last + 1], (1, M_CHUNK))

        log_d = jnp.where(mask, b_row + r_cols[:, h:h + 1], -jnp.inf)
        log_inter = b_row + m_prev
        m_t = jnp.maximum(jnp.max(log_d, axis=0, keepdims=True), log_inter)
        st = lax.dot_general(kh, qh, nt, preferred_element_type=F32)
        st = st * jnp.exp(log_d - m_t)
        w_inter = jnp.exp(log_inter - m_t)
        c_prev = c_ref[d, h]
        n_prev = n_ref[d, h]
        cq = lax.dot_general(c_prev.astype(BF16), qh, nt, preferred_element_type=F32)
        num = jnp.dot(vt.astype(BF16), st.astype(BF16), preferred_element_type=F32) + w_inter * cq
        qn = lax.dot_general(n_prev.astype(BF16), qh, nt, preferred_element_type=F32)[0:1, :]
        den = jnp.sum(st, axis=0, keepdims=True) + w_inter * qn
        o_ref[:, sl] = (num / jnp.maximum(jnp.abs(den), jnp.exp(-m_t))).T

        log_w = b_last - b_row + i_rows[h:h + 1, :]
        m_new = jnp.maximum(b_last + m_prev, jnp.max(log_w, axis=1, keepdims=True))
        w = jnp.exp(log_w - m_new)
        decay = jnp.exp(b_last + m_prev - m_new)
        vw = (vt * w).astype(BF16)
        c_ref[d, h] = decay[:, 0:1] * c_prev + jnp.dot(vw, kh, preferred_element_type=F32)
        w8 = jnp.broadcast_to(w, (SUBLANES, M_CHUNK)).astype(BF16)
        n_ref[d, h] = decay[:, 0:1] * n_prev + jnp.dot(w8, kh, preferred_element_type=F32)
        m_ref[d, h:h + 1, :] = m_new


def _mlstm_kernel(qf_ref, kf_ref, vf_ref, gf_ref, qb_ref, kb_ref, vb_ref, gb_ref, bias_ref, of_ref, ob_ref,
                  c_ref, n_ref, m_ref):
    @pl.when(pl.program_id(1) == 0)
    def _():
        c_ref[...] = jnp.zeros(c_ref.shape, F32)
        n_ref[...] = jnp.zeros(n_ref.shape, F32)
        m_ref[...] = jnp.zeros(m_ref.shape, F32)

    bias = bias_ref[...]
    _mlstm_direction(qf_ref, kf_ref, vf_ref, gf_ref[...] + bias, of_ref, c_ref, n_ref, m_ref, 0)
    _mlstm_direction(qb_ref, kb_ref, vb_ref, gb_ref[...] + bias, ob_ref, c_ref, n_ref, m_ref, 1)


def _chunk_order(i, reverse):
    if not reverse:
        return i
    nctx = CTX_LEN // M_CHUNK
    ntot = SEQ_ALL // M_CHUNK
    return jnp.where(i < nctx, nctx - 1 - i, ntot + nctx - 1 - i)


def _token_chunk(b, i, reverse):
    nctx = CTX_LEN // M_CHUNK
    ci = _chunk_order(i, reverse)
    return jnp.where(ci < nctx, N_LAT // M_CHUNK + b * nctx + ci, b * (SEQ // M_CHUNK) + ci - nctx)


def _mlstm(q_and_k, p, g, gate_b):
    def blk(col, reverse):
        return pl.BlockSpec((None, M_CHUNK, W_M), lambda b, i: (b, _chunk_order(i, reverse), col))

    def tok(col, reverse, width=W_M):
        return pl.BlockSpec((M_CHUNK, width), lambda b, i: (_token_chunk(b, i, reverse), col))

    in_specs = [blk(0, False), blk(1, False), tok(COL_V, False), tok(0, False, LANES),
                blk(0, True), blk(1, True), tok(COL_V, True), tok(0, True, LANES),
                pl.BlockSpec((1, LANES), lambda b, i: (0, 0))]
    out = jax.ShapeDtypeStruct((N_TOK, W_M), F32)
    return pl.pallas_call(
        _mlstm_kernel,
        grid=(BATCH, SEQ_ALL // M_CHUNK),
        in_specs=in_specs,
        out_specs=[tok(0, False), tok(0, True)],
        out_shape=[out, out],
        scratch_shapes=[pltpu.VMEM((2, M_HEADS, M_HEAD_DIM, M_HEAD_DIM), F32),
                        pltpu.VMEM((2, M_HEADS, SUBLANES, M_HEAD_DIM), F32),
                        pltpu.VMEM((2, M_HEADS, LANES), F32)],
        compiler_params=_cparams("arbitrary", "arbitrary"),
        name="mlstm",
    )(q_and_k, q_and_k, p, g, q_and_k, q_and_k, p, g, gate_b)


def _merge_kernel(yr_ref, hf_ref, hb_ref, o_ref, ga_ref, gb_ref, gn_ref, prg_ref, pm_ref, mix_ref):
    hs = hf_ref[...] + hb_ref[...]
    hn = jnp.concatenate([_layer_norm(hs[:, h * M_HEAD_DIM:(h + 1) * M_HEAD_DIM]) for h in range(M_HEADS)], axis=1)
    ym = (hn * gn_ref[...] * _sigmoid(o_ref[...])).astype(BF16)
    a = jnp.dot(yr_ref[...], prg_ref[...], preferred_element_type=F32)
    b = jnp.dot(ym, pm_ref[...], preferred_element_type=F32)
    mix_ref[...] = (_sigmoid(ga_ref[...]) * a + _sigmoid(gb_ref[...]) * b).astype(BF16)


def _merge(yr, hf, hb, p1, p2, gn, p_rg, p_m, rows):
    tm = 256
    once = pl.Buffered(1)
    return pl.pallas_call(
        _merge_kernel,
        grid=(rows // tm,),
        in_specs=[pl.BlockSpec((tm, W_RG), lambda i: (i, 0)),
                  pl.BlockSpec((tm, W_M), lambda i: (i, 0)),
                  pl.BlockSpec((tm, W_M), lambda i: (i, 0)),
                  pl.BlockSpec((tm, W_M), lambda i: (i, COL_O)),
                  pl.BlockSpec((tm, D_MODEL), lambda i: (i, 0)),
                  pl.BlockSpec((tm, D_MODEL), lambda i: (i, 1)),
                  pl.BlockSpec((1, W_M), lambda i: (0, 0)),
                  pl.BlockSpec((W_RG, D_MODEL), lambda i: (0, 0), pipeline_mode=once),
                  pl.BlockSpec((W_M, D_MODEL), lambda i: (0, 0), pipeline_mode=once)],
        out_specs=pl.BlockSpec((tm, D_MODEL), lambda i: (i, 0)),
        out_shape=jax.ShapeDtypeStruct((rows, D_MODEL), BF16),
        compiler_params=_cparams("arbitrary"),
        name="merge",
    )(yr, hf, hb, p1, p2, p2, gn, p_rg, p_m)


def _res_ln(h, f, gate, lng, lnb):
    y = _layer_norm(ALPHA * h + gate * f)
    return y * lng + lnb


def _outproj_kernel(a_ref, w_ref, h_ref, g_ref, lng_ref, lnb_ref, o_ref):
    f = jnp.dot(a_ref[...], w_ref[...], preferred_element_type=F32)
    o_ref[...] = _res_ln(h_ref[...], f, g_ref[0], lng_ref[...], lnb_ref[...])


def _outproj(a, w, h, mod, gate_chunk, lng, lnb, rows):
    tm = 256
    return pl.pallas_call(
        _outproj_kernel,
        grid=(rows // tm,),
        in_specs=[pl.BlockSpec((tm, D_MODEL), lambda i: (i, 0)),
                  pl.BlockSpec((D_MODEL, D_MODEL), lambda i: (0, 0)),
                  pl.BlockSpec((tm, D_MODEL), lambda i: (i, 0)),
                  _mod_spec(gate_chunk, tm),
                  pl.BlockSpec((1, D_MODEL), lambda i: (0, 0)),
                  pl.BlockSpec((1, D_MODEL), lambda i: (0, 0))],
        out_specs=pl.BlockSpec((tm, D_MODEL), lambda i: (i, 0)),
        out_shape=jax.ShapeDtypeStruct((rows, D_MODEL), F32),
        compiler_params=_cparams("arbitrary"),
        name="out_proj",
    )(a, w, h, mod, lng, lnb)


def _ffn_kernel(h_ref, sh_ref, sc_ref, g_ref, w1_ref, w3_ref, w2_ref, lng_ref, lnb_ref, o_ref, u_ref, acc_ref):
    f = pl.program_id(1)

    @pl.when(f == 0)
    def _():
        u = _layer_norm(h_ref[...]) * (1.0 + sc_ref[0]) + sh_ref[0]
        u_ref[...] = u.astype(BF16)
        acc_ref[...] = jnp.zeros(acc_ref.shape, F32)

    u = u_ref[...]
    h1 = jnp.dot(u, w1_ref[...], preferred_element_type=F32)
    h3 = jnp.dot(u, w3_ref[...], preferred_element_type=F32)
    act = (h1 * _sigmoid(h1) * h3).astype(BF16)
    acc_ref[...] += jnp.dot(act, w2_ref[...], preferred_element_type=F32)

    @pl.when(f == pl.num_programs(1) - 1)
    def _():
        o_ref[...] = _res_ln(h_ref[...], acc_ref[...], g_ref[0], lng_ref[...], lnb_ref[...])


def _ffn(h, mod, w1, w3, w2, lng, lnb):
    rows = h.shape[0]
    tm, tf = 512, 512
    return pl.pallas_call(
        _ffn_kernel,
        grid=(rows // tm, D_FF // tf),
        in_specs=[pl.BlockSpec((tm, D_MODEL), lambda i, f: (i, 0)),
                  _mod_spec(3, tm), _mod_spec(4, tm), _mod_spec(5, tm),
                  pl.BlockSpec((D_MODEL, tf), lambda i, f: (0, f)),
                  pl.BlockSpec((D_MODEL, tf), lambda i, f: (0, f)),
                  pl.BlockSpec((tf, D_MODEL), lambda i, f: (f, 0)),
                  pl.BlockSpec((1, D_MODEL), lambda i, f: (0, 0)),
                  pl.BlockSpec((1, D_MODEL), lambda i, f: (0, 0))],
        out_specs=pl.BlockSpec((tm, D_MODEL), lambda i, f: (i, 0)),
        out_shape=jax.ShapeDtypeStruct((rows, D_MODEL), F32),
        scratch_shapes=[pltpu.VMEM((tm, D_MODEL), BF16), pltpu.VMEM((tm, D_MODEL), F32)],
        compiler_params=_cparams("arbitrary", "arbitrary"),
        name="ffn_dense",
    )(h, mod, mod, mod, w1, w3, w2, lng, lnb)


def _route_kernel(h_ref, sh_ref, sc_ref, rw_ref, rb_ref, t_ref, idx_ref, gate_ref):
    u = _layer_norm(h_ref[...]) * (1.0 + sc_ref[0]) + sh_ref[0]
    u = u.astype(BF16)
    t_ref[...] = u.astype(F32)
    lane = lax.broadcasted_iota(jnp.int32, (h_ref.shape[0], LANES), 1)
    logits = jnp.dot(u, rw_ref[...], preferred_element_type=F32) + rb_ref[...]
    logits = jnp.where(lane < N_EXPERTS, logits, -jnp.inf)
    v1 = jnp.max(logits, axis=-1, keepdims=True)
    i1 = jnp.min(jnp.where(logits == v1, lane, LANES), axis=-1, keepdims=True)
    rest = jnp.where(lane == i1, -jnp.inf, logits)
    v2 = jnp.max(rest, axis=-1, keepdims=True)
    i2 = jnp.min(jnp.where(rest == v2, lane, LANES), axis=-1, keepdims=True)
    e2 = jnp.exp(v2 - v1)
    g1 = 1.0 / (1.0 + e2)
    idx_ref[...] = jnp.where(lane == 0, i1, jnp.where(lane == 1, i2, 0))
    gate_ref[...] = jnp.where(lane == 0, g1, jnp.where(lane == 1, e2 * g1, 0.0))


def _route(h, mod, rw, rb):
    rows = h.shape[0]
    tm = 512
    return pl.pallas_call(
        _route_kernel,
        grid=(rows // tm,),
        in_specs=[pl.BlockSpec((tm, D_MODEL), lambda i: (i, 0)),
                  _mod_spec(3, tm), _mod_spec(4, tm),
                  pl.BlockSpec((D_MODEL, LANES), lambda i: (0, 0)),
                  pl.BlockSpec((1, LANES), lambda i: (0, 0))],
        out_specs=[pl.BlockSpec((tm, D_MODEL), lambda i: (i, 0)),
                   pl.BlockSpec((tm, LANES), lambda i: (i, 0)),
                   pl.BlockSpec((tm, LANES), lambda i: (i, 0))],
        out_shape=[jax.ShapeDtypeStruct((rows, D_MODEL), F32), jax.ShapeDtypeStruct((rows, LANES), jnp.int32),
                   jax.ShapeDtypeStruct((rows, LANES), F32)],
        compiler_params=_cparams("arbitrary"),
        name="moe_route",
    )(h, mod, mod, rw, rb)


def _gather_start(src_hbm, idx_ref, idx0, dst_ref, n, sem):
    def issue(r, carry):
        row = idx_ref[0, 0, idx0 + r]
        pltpu.make_async_copy(src_hbm.at[pl.ds(row, 1)], dst_ref.at[pl.ds(r, 1)], sem).start()
        return carry

    lax.fori_loop(0, n, issue, 0)


def _gather_wait(src_hbm, dst_ref, n, sem):
    pltpu.make_async_copy(src_hbm.at[pl.ds(0, n)], dst_ref, sem).wait()


def _experts_kernel(te_ref, tv_ref, tok_ref, nxt_ref, t_hbm, w1_ref, w3_ref, w2_ref, o_ref,
                    xf_ref, x_ref, acc_ref, sem):
    i = pl.program_id(0)
    f = pl.program_id(1)
    last_tile = pl.num_programs(0) - 1
    live = tv_ref[i] > 0
    next_live = jnp.logical_and(i < last_tile, tv_ref[jnp.minimum(i + 1, last_tile)] > 0)
    slot = i % 2

    @pl.when(jnp.logical_and(live, jnp.logical_and(f == 0, i == 0)))
    def _():
        _gather_start(t_hbm, tok_ref, 0, xf_ref.at[0], MOE_TILE, sem.at[0])

    @pl.when(jnp.logical_and(live, f == 0))
    def _():
        _gather_wait(t_hbm, xf_ref.at[slot], MOE_TILE, sem.at[slot])
        x_ref[...] = xf_ref[slot].astype(BF16)
        acc_ref[...] = jnp.zeros(acc_ref.shape, F32)

    @pl.when(jnp.logical_and(next_live, f == 0))
    def _():
        _gather_start(t_hbm, nxt_ref, 0, xf_ref.at[1 - slot], MOE_TILE, sem.at[1 - slot])

    @pl.when(live)
    def _():
        x = x_ref[...]
        h1 = jnp.dot(x, w1_ref[...], preferred_element_type=F32)
        h3 = jnp.dot(x, w3_ref[...], preferred_element_type=F32)
        act = (h1 * _sigmoid(h1) * h3).astype(BF16)
        acc_ref[...] += jnp.dot(act, w2_ref[...], preferred_element_type=F32)

    @pl.when(f == pl.num_programs(1) - 1)
    def _():
        o_ref[...] = jnp.where(live, acc_ref[...], 0.0)


def _experts(tile_expert, tile_live, row_token, t, w1, w3, w2):
    tf = 512
    grid_spec = pltpu.PrefetchScalarGridSpec(
        num_scalar_prefetch=2,
        grid=(MOE_TILES, D_FF // tf),
        in_specs=[pl.BlockSpec((1, 1, MOE_TILE), lambda i, f, te, tv: (i, 0, 0), memory_space=pltpu.SMEM),
                  pl.BlockSpec((1, 1, MOE_TILE), lambda i, f, te, tv: (jnp.minimum(i + 1, MOE_TILES - 1), 0, 0),
                               memory_space=pltpu.SMEM),
                  pl.BlockSpec(memory_space=pl.ANY),
                  pl.BlockSpec((None, D_MODEL, tf), lambda i, f, te, tv: (te[i], 0, f)),
                  pl.BlockSpec((None, D_MODEL, tf), lambda i, f, te, tv: (te[i], 0, f)),
                  pl.BlockSpec((None, tf, D_MODEL), lambda i, f, te, tv: (te[i], f, 0))],
        out_specs=pl.BlockSpec((MOE_TILE, D_MODEL), lambda i, f, te, tv: (i, 0)),
        scratch_shapes=[pltpu.VMEM((2, MOE_TILE, D_MODEL), F32), pltpu.VMEM((MOE_TILE, D_MODEL), BF16),
                        pltpu.VMEM((MOE_TILE, D_MODEL), F32), pltpu.SemaphoreType.DMA((2,))],
    )
    return pl.pallas_call(
        _experts_kernel,
        grid_spec=grid_spec,
        out_shape=jax.ShapeDtypeStruct((MOE_TILES * MOE_TILE, D_MODEL), F32),
        compiler_params=_cparams("arbitrary", "arbitrary"),
        name="moe_experts",
    )(tile_expert, tile_live, row_token, row_token, t, w1, w3, w2)


def _combine_kernel(pos_ref, nxt_ref, y_hbm, g_ref, h_ref, gm_ref, lng_ref, lnb_ref, o_ref, buf_ref, sem):
    i = pl.program_id(0)
    tm = h_ref.shape[0]
    slot = i % 2

    @pl.when(i == 0)
    def _():
        _gather_start(y_hbm, pos_ref, 0, buf_ref.at[0], 2 * tm, sem.at[0])

    @pl.when(i < pl.num_programs(0) - 1)
    def _():
        _gather_start(y_hbm, nxt_ref, 0, buf_ref.at[1 - slot], 2 * tm, sem.at[1 - slot])

    _gather_wait(y_hbm, buf_ref.at[slot], 2 * tm, sem.at[slot])
    g = g_ref[...]
    f = g[:, 0:1] * buf_ref[slot, 0:tm, :] + g[:, 1:2] * buf_ref[slot, tm:2 * tm, :]
    o_ref[...] = _res_ln(h_ref[...], f, gm_ref[0], lng_ref[...], lnb_ref[...])


def _combine(pos, yb, gates, h, mod, lng, lnb):
    rows = h.shape[0]
    tm = pos.shape[-1] // 2
    nt = rows // tm
    return pl.pallas_call(
        _combine_kernel,
        grid=(nt,),
        in_specs=[pl.BlockSpec((1, 1, 2 * tm), lambda i: (i, 0, 0), memory_space=pltpu.SMEM),
                  pl.BlockSpec((1, 1, 2 * tm), lambda i: (jnp.minimum(i + 1, nt - 1), 0, 0), memory_space=pltpu.SMEM),
                  pl.BlockSpec(memory_space=pl.ANY),
                  pl.BlockSpec((tm, LANES), lambda i: (i, 0)),
                  pl.BlockSpec((tm, D_MODEL), lambda i: (i, 0)),
                  _mod_spec(5, tm),
                  pl.BlockSpec((1, D_MODEL), lambda i: (0, 0)),
                  pl.BlockSpec((1, D_MODEL), lambda i: (0, 0))],
        out_specs=pl.BlockSpec((tm, D_MODEL), lambda i: (i, 0)),
        out_shape=jax.ShapeDtypeStruct((rows, D_MODEL), F32),
        scratch_shapes=[pltpu.VMEM((2, 2 * tm, D_MODEL), F32), pltpu.SemaphoreType.DMA((2,))],
        compiler_params=_cparams("arbitrary"),
        name="moe_combine",
    )(pos, pos, yb, gates, h, mod, lng, lnb)


def _cast_kernel(w_ref, o_ref):
    o_ref[...] = w_ref[...].astype(BF16)


def _to_bf16(w, tr):
    r, c = w.shape
    return pl.pallas_call(
        _cast_kernel,
        grid=(r // tr,),
        in_specs=[pl.BlockSpec((tr, c), lambda i: (i, 0))],
        out_specs=pl.BlockSpec((tr, c), lambda i: (i, 0)),
        out_shape=jax.ShapeDtypeStruct((r, c), BF16),
        compiler_params=_cparams("arbitrary"),
        name="cast_bf16",
    )(w)


def _moe(h, mod, router_w, router_b, w1, w3, w2, lng, lnb):
    n = h.shape[0]
    rw = jnp.zeros((D_MODEL, LANES), BF16).at[:, :N_EXPERTS].set(router_w.astype(BF16))
    rb = jnp.zeros((1, LANES), F32).at[0, :N_EXPERTS].set(router_b)
    t, idx, gates = _route(h, mod, rw, rb)
    expert = idx[:, :TOP_K].reshape(-1)
    onehot = (expert[:, None] == jnp.arange(N_EXPERTS)[None, :]).astype(jnp.int32)
    csum = jnp.cumsum(onehot, axis=0)
    rank = jnp.sum(csum * onehot, axis=1) - 1
    counts = csum[-1]
    padded = (counts + MOE_TILE - 1) // MOE_TILE * MOE_TILE
    pend = jnp.cumsum(padded)
    pstart = pend - padded
    pos = pstart[expert] + rank
    tile_start = jnp.arange(MOE_TILES) * MOE_TILE
    tile_expert = jnp.minimum(jnp.searchsorted(pend, tile_start, side='right'), N_EXPERTS - 1).astype(jnp.int32)
    tile_live = (tile_start < pend[-1]).astype(jnp.int32)
    order = jnp.argsort(expert, stable=True).astype(jnp.int32)
    row_expert = jnp.repeat(tile_expert, MOE_TILE)
    k = jnp.arange(MOE_TILES * MOE_TILE, dtype=jnp.int32) - pstart[row_expert].astype(jnp.int32)
    sorted_at = jnp.clip((jnp.cumsum(counts) - counts)[row_expert].astype(jnp.int32) + k, 0, n * TOP_K - 1)
    row_token = jnp.where(k < counts[row_expert], order[sorted_at] // TOP_K, 0).astype(jnp.int32)
    yb = _experts(tile_expert, tile_live, row_token.reshape(MOE_TILES, 1, MOE_TILE), t, w1, w3, w2)
    tm = 256
    pos2 = pos.reshape(n // tm, tm, TOP_K).transpose(0, 2, 1).reshape(n // tm, 1, TOP_K * tm).astype(jnp.int32)
    return _combine(pos2, yb, gates, h, mod, lng, lnb)


def _swap_grid_order(t):
    b, l, c = t.shape
    return t.reshape(b, l // GRID_W, GRID_W, c).transpose(0, 2, 1, 3).reshape(b, l, c)


def _mixer(h, mod, w_in_all, layer, conv_rg_w, conv_rg_b, conv_m_w, conv_m_b, rg_wa, rg_ba, rg_wx, rg_bx, rg_lam,
           m_gate_b, m_gn_g, p_rg, p_m, rows_out):
    gate0 = C_M + 4 * W_M
    w_tail = w_in_all[layer, :, gate0:]
    w_gate = jnp.zeros((D_MODEL, LANES), BF16).at[:, :N_GATES].set(w_tail[:, :N_GATES].astype(BF16))
    u, pg = _lnmod(h, mod, w_gate)
    p = _proj(u, w_in_all[layer:layer + 1, :, :gate0], 0, gate0)
    p2 = _proj(u, w_tail[None, :, N_GATES:], 0, 2 * D_MODEL)

    wcat = (0.5 * jnp.concatenate([rg_wa[0], rg_wx[0], rg_wa[1], rg_wx[1]], axis=-1)).astype(BF16)
    bcat = 0.5 * jnp.stack([rg_ba[0], rg_bx[0], rg_ba[1], rg_bx[1]], axis=0)
    bcat = bcat.reshape(4, RG_BLOCKS, RG_BLOCK).transpose(1, 0, 2).reshape(RG_BLOCKS, 1, 4 * RG_BLOCK)
    cb = conv_rg_b.reshape(1, W_RG)
    h0 = jnp.zeros((BATCH, 2, W_RG), F32)
    yr_c, st = _rg_branch(p, h0, conv_rg_w, cb, wcat, bcat, rg_lam, seq=CTX_LEN, row_block0=N_LAT // CTX_LEN,
                          colmajor=False)
    yr, _ = _rg_branch(p, st, conv_rg_w, cb, wcat, bcat, rg_lam, seq=SEQ, row_block0=0, colmajor=True)
    if rows_out > N_LAT:
        yr = jnp.concatenate([yr, yr_c], axis=0)

    qscale = jnp.concatenate([jnp.full((1, W_M), M_HEAD_DIM ** -0.5, F32), jnp.ones((1, W_M), F32)], axis=1)
    qk = _qkconv(p, conv_m_w, conv_m_b.reshape(1, 2 * W_M), qscale)
    gate_b = jnp.zeros((1, LANES), F32).at[0, :N_GATES].set(m_gate_b.reshape(-1))
    h_f, h_b = _mlstm(qk, p, pg, gate_b)

    return _merge(yr, h_f, h_b, p, p2, m_gn_g.reshape(1, W_M), p_rg.astype(BF16), p_m.astype(BF16), rows_out)


def kernel(x, c, ctx, c_ctx, w_mod, b_mod, w_in, conv_rg_w, conv_rg_b, conv_m_w, conv_m_b, rg_wa, rg_ba, rg_wx, rg_bx, rg_lam, m_gate_b, m_gn_g, p_rg, p_m, w_out, ln_g, ln_b, ff_w1, ff_w3, ff_w2, router_w, router_b, ex_w1, ex_w3, ex_w2):
    h = jnp.concatenate([_swap_grid_order(x).reshape(N_LAT, D_MODEL), ctx.reshape(N_CTX, D_MODEL)], axis=0)
    cond = jnp.zeros((SUBLANES, D_MODEL), F32).at[:BATCH].set(c).at[BATCH].set(c_ctx)
    for l in range(DEPTH):
        last = l == DEPTH - 1
        rows = N_LAT if last else N_TOK
        mod = _modvec(cond, w_mod, b_mod[l], l)
        mix = _mixer(h, mod, w_in, l, conv_rg_w[l], conv_rg_b[l], conv_m_w[l], conv_m_b[l], rg_wa[l], rg_ba[l],
                     rg_wx[l], rg_bx[l], rg_lam[l], m_gate_b[l], m_gn_g[l], p_rg[l], p_m[l], rows)
        lng = ln_g[l].reshape(2, 1, D_MODEL)
        lnb = ln_b[l].reshape(2, 1, D_MODEL)
        h = _outproj(mix, w_out[l].astype(BF16), h, mod, 2, lng[0], lnb[0], rows)
        k = l // 2
        if l % 2 == 0:
            h = _ffn(h, mod, _to_bf16(ff_w1[k], 256), _to_bf16(ff_w3[k], 256), _to_bf16(ff_w2[k], 1024),
                     lng[1], lnb[1])
        else:
            w1 = _to_bf16(ex_w1[k].reshape(N_EXPERTS * D_MODEL, D_FF), 256).reshape(N_EXPERTS, D_MODEL, D_FF)
            w3 = _to_bf16(ex_w3[k].reshape(N_EXPERTS * D_MODEL, D_FF), 256).reshape(N_EXPERTS, D_MODEL, D_FF)
            w2 = _to_bf16(ex_w2[k].reshape(N_EXPERTS * D_FF, D_MODEL), 1024).reshape(N_EXPERTS, D_FF, D_MODEL)
            h = _moe(h, mod, router_w[k], router_b[k], w1, w3, w2, lng[1], lnb[1])
    return _swap_grid_order(h[:N_LAT].reshape(BATCH, SEQ, D_MODEL))
```

```python
import functools

import jax
import jax.numpy as jnp
from jax import lax
from jax.experimental import pallas as pl
from jax.experimental.pallas import tpu as pltpu

F32 = jnp.float32
BF16 = jnp.bfloat16

D_MODEL = 2048
BATCH = 4
SEQ = 4096
DEPTH = 2
GRID_W = 64
CTX_LEN = 256
W_RG = 2048
RG_BLOCKS = 16
RG_BLOCK = W_RG // RG_BLOCKS
RG_C = 8.0
CONV_W = 4
M_HEADS = 8
M_HEAD_DIM = 256
W_M = M_HEADS * M_HEAD_DIM
M_CHUNK = 128
C_M = 2 * W_RG
N_GATES = 4 * M_HEADS
D_FF = 7168
N_EXPERTS = 8
TOP_K = 2
ALPHA = (2.0 * DEPTH) ** 0.25
LN_EPS = 1e-6

N_LAT = BATCH * SEQ
N_CTX = BATCH * CTX_LEN
N_TOK = N_LAT + N_CTX
LANES = 128
SUBLANES = 8
VMEM_LIMIT = 52 * 1024 * 1024

COL_XR, COL_RGG, COL_Q, COL_K, COL_V, COL_O = range(6)

SEQ_ALL = CTX_LEN + SEQ
CONV_PAD = 8
MOE_TILE = 512
MOE_TILES = (N_LAT * TOP_K) // MOE_TILE + N_EXPERTS


def _cparams(*sem):
    return pltpu.CompilerParams(dimension_semantics=sem, vmem_limit_bytes=VMEM_LIMIT)


def _layer_norm(x):
    mu = jnp.mean(x, axis=-1, keepdims=True)
    xc = x - mu
    var = jnp.mean(xc * xc, axis=-1, keepdims=True)
    return xc * lax.rsqrt(var + LN_EPS)


def _sigmoid(x):
    return 0.5 * jnp.tanh(0.5 * x) + 0.5


def _softplus(x):
    return jnp.maximum(x, 0.0) + jnp.log1p(jnp.exp(-jnp.abs(x)))


def _mod_row(i, tm):
    return jnp.where(i >= N_LAT // tm, BATCH, i // (SEQ // tm))


def _mod_spec(chunk, tm):
    return pl.BlockSpec((1, 1, D_MODEL), lambda i, *_: (_mod_row(i, tm), 0, chunk))


def _modvec_kernel(s_ref, w_ref, b_ref, o_ref):
    s = s_ref[...]
    s = s * _sigmoid(s)
    o_ref[...] = jnp.dot(s.astype(BF16), w_ref[...].astype(BF16), preferred_element_type=F32) + b_ref[...]


def _modvec(cond, w, b, layer):
    n = w.shape[-1]
    tn = 1024
    out = pl.pallas_call(
        _modvec_kernel,
        grid=(n // tn,),
        in_specs=[pl.BlockSpec((SUBLANES, D_MODEL), lambda j: (0, 0)),
                  pl.BlockSpec((None, D_MODEL, tn), lambda j: (layer, 0, j)),
                  pl.BlockSpec((1, tn), lambda j: (0, j))],
        out_specs=pl.BlockSpec((SUBLANES, tn), lambda j: (0, j)),
        out_shape=jax.ShapeDtypeStruct((SUBLANES, n), F32),
        compiler_params=_cparams("arbitrary"),
        name="modvec",
    )(cond, w, b.reshape(1, n))
    return out.reshape(SUBLANES, 1, n)


def _lnmod_kernel(x_ref, sh_ref, sc_ref, wg_ref, u_ref, g_ref):
    u = (_layer_norm(x_ref[...]) * (1.0 + sc_ref[0]) + sh_ref[0]).astype(BF16)
    u_ref[...] = u
    g_ref[...] = jnp.dot(u, wg_ref[...], preferred_element_type=F32)


def _lnmod(h, mod, wg):
    r = h.shape[0]
    tm = 512
    return pl.pallas_call(
        _lnmod_kernel,
        grid=(r // tm,),
        in_specs=[pl.BlockSpec((tm, D_MODEL), lambda i: (i, 0)),
                  _mod_spec(0, tm), _mod_spec(1, tm),
                  pl.BlockSpec((D_MODEL, LANES), lambda i: (0, 0))],
        out_specs=[pl.BlockSpec((tm, D_MODEL), lambda i: (i, 0)),
                   pl.BlockSpec((tm, LANES), lambda i: (i, 0))],
        out_shape=[jax.ShapeDtypeStruct((r, D_MODEL), BF16), jax.ShapeDtypeStruct((r, LANES), F32)],
        compiler_params=_cparams("arbitrary"),
        name="ln_mod",
    )(h, mod, mod, wg)


def _proj_kernel(u_ref, w_ref, o_ref, wb_ref):
    @pl.when(pl.program_id(1) == 0)
    def _():
        wb_ref[...] = w_ref[...].astype(BF16)

    o_ref[...] = jnp.dot(u_ref[...], wb_ref[...], preferred_element_type=F32)


def _proj(u, w, layer, n):
    r = u.shape[0]
    tm, tn = 1024, 1024
    return pl.pallas_call(
        _proj_kernel,
        grid=(n // tn, r // tm),
        in_specs=[pl.BlockSpec((tm, D_MODEL), lambda j, i: (i, 0)),
                  pl.BlockSpec((None, D_MODEL, tn), lambda j, i: (layer, 0, j))],
        out_specs=pl.BlockSpec((tm, tn), lambda j, i: (i, j)),
        out_shape=jax.ShapeDtypeStruct((r, n), F32),
        scratch_shapes=[pltpu.VMEM((D_MODEL, tn), BF16)],
        compiler_params=_cparams("arbitrary", "arbitrary"),
        name="proj_in",
    )(u, w)


def _rg_coeffs(zh, hx, sp4, d):
    tr = jnp.tanh(zh[:, (2 * d) * LANES:(2 * d + 1) * LANES])
    ti = jnp.tanh(zh[:, (2 * d + 1) * LANES:(2 * d + 2) * LANES])
    neg_log_a = sp4[d:d + 1, :] * (tr + 1.0)
    a = jnp.exp(-neg_log_a)
    return a, jnp.sqrt(jnp.tanh(neg_log_a) * (1.0 + a * a)) * ((ti + 1.0) * hx)


def _rg_carries(h0_ref, hl_ref, hf, pf, hb, pb):
    h0 = h0_ref[...]
    cf = [None] * SUBLANES
    cbk = [None] * SUBLANES
    carry = h0[0:1, :]
    for s in range(SUBLANES):
        cf[s] = carry
        carry = pf[s:s + 1, :] * carry + hf[s:s + 1, :]
    hl_f = carry
    carry = h0[1:2, :]
    for s in reversed(range(SUBLANES)):
        cbk[s] = carry
        carry = pb[s:s + 1, :] * carry + hb[s:s + 1, :]
    hl_ref[...] = jnp.concatenate([hl_f, carry], axis=0)
    return cf, cbk


def _rg_kernel(x_ref, g_ref, cw_ref, cb_ref, w_ref, bias_ref, lam_ref, h0_ref, y_ref, hl_ref,
               xs, a0, b0, a1, b1, *, seq, tc, seg, pitch):
    zeros8 = jnp.zeros((CONV_PAD, LANES), F32)
    xs[0:CONV_PAD, :] = zeros8
    xs[CONV_PAD + seq:CONV_PAD + seq + CONV_PAD, :] = zeros8
    xs[CONV_PAD:CONV_PAD + seq, :] = x_ref[...]

    cw = cw_ref[...]
    sp4 = (0.5 * RG_C) * _softplus(-lam_ref[...])
    dec = ((a0, b0), (a1, b1))
    for c in range(seq // tc):
        r0 = c * tc
        xc = cb_ref[...] + cw[0:1, :] * xs[r0 + CONV_PAD - 2:r0 + CONV_PAD - 2 + tc, :]
        for j in range(1, CONV_W):
            xc = xc + cw[j:j + 1, :] * xs[r0 + CONV_PAD - 2 + j:r0 + CONV_PAD - 2 + j + tc, :]
        zh = jnp.dot(xc.astype(BF16), w_ref[...], preferred_element_type=F32) + bias_ref[...]
        hx = 0.5 * xc
        for d in range(2):
            a, b = _rg_coeffs(zh, hx, sp4, d)
            a_ref, b_ref = dec[d]
            for p in range(max(tc // seg, 1)):
                n = min(tc, seg)
                t = r0 + p * n
                dst = (t // seg) * pitch + t % seg
                a_ref[dst:dst + n, :] = a[p * n:(p + 1) * n, :]
                b_ref[dst:dst + n, :] = b[p * n:(p + 1) * n, :]

    def scan_step(i, carry):
        hf, pf, hb, pb = carry
        tf = i
        tb = seg - 1 - i
        af = a0[pl.ds(tf, SUBLANES, stride=pitch), :]
        bf = b0[pl.ds(tf, SUBLANES, stride=pitch), :]
        ab = a1[pl.ds(tb, SUBLANES, stride=pitch), :]
        bb = b1[pl.ds(tb, SUBLANES, stride=pitch), :]
        hf = af * hf + bf
        pf = pf * af
        hb = ab * hb + bb
        pb = pb * ab
        b0[pl.ds(tf, SUBLANES, stride=pitch), :] = hf
        a0[pl.ds(tf, SUBLANES, stride=pitch), :] = pf
        b1[pl.ds(tb, SUBLANES, stride=pitch), :] = hb
        a1[pl.ds(tb, SUBLANES, stride=pitch), :] = pb
        return hf, pf, hb, pb

    z8 = jnp.zeros((SUBLANES, LANES), F32)
    o8 = jnp.ones((SUBLANES, LANES), F32)
    hf, pf, hb, pb = lax.fori_loop(0, seg, scan_step, (z8, o8, z8, o8), unroll=4)

    cf, cbk = _rg_carries(h0_ref, hl_ref, hf, pf, hb, pb)
    for s in range(SUBLANES):
        for c in range(max(seg // tc, 1)):
            n = min(tc, seg)
            src = s * pitch + c * n
            t = s * seg + c * n
            h = (b0[src:src + n, :] + a0[src:src + n, :] * cf[s]
                 + b1[src:src + n, :] + a1[src:src + n, :] * cbk[s])
            y_ref[t:t + n, :] = (h * jax.nn.gelu(g_ref[t:t + n, :])).astype(y_ref.dtype)


def _rg_grid_kernel(x_ref, g_ref, cw_ref, cb_ref, w_ref, bias_ref, lam_ref, h0_ref, y_ref, hl_ref, a0, b0, a1, b1):
    n_rows = x_ref.shape[0] // GRID_W
    tc = 4 * n_rows
    cw = cw_ref[...]
    sp4 = (0.5 * RG_C) * _softplus(-lam_ref[...])
    dec = ((a0, b0), (a1, b1))
    rid = lax.broadcasted_iota(jnp.int32, (n_rows, LANES), 0)

    def column(c, shift):
        if shift == 0:
            return x_ref[c * n_rows:(c + 1) * n_rows, :]
        if shift < 0:
            v = x_ref[c * n_rows - 1:(c + 1) * n_rows - 1, :]
            return jnp.where(rid == 0, 0.0, v)
        v = x_ref[c * n_rows + 1:(c + 1) * n_rows + 1, :]
        return jnp.where(rid == n_rows - 1, 0.0, v)

    def tap(c0, off):
        lo = c0 + off
        if lo >= 0 and lo + 4 <= GRID_W:
            return x_ref[lo * n_rows:(lo + 4) * n_rows, :]
        parts = []
        for c in range(lo, lo + 4):
            if c < 0:
                parts.append(column(c + GRID_W, -1))
            elif c >= GRID_W:
                parts.append(column(c - GRID_W, 1))
            else:
                parts.append(column(c, 0))
        return jnp.concatenate(parts, axis=0)

    for k in range(GRID_W // 4):
        c0 = 4 * k
        xc = cb_ref[...] + cw[0:1, :] * tap(c0, -2)
        for j in range(1, CONV_W):
            xc = xc + cw[j:j + 1, :] * tap(c0, j - 2)
        zh = jnp.dot(xc.astype(BF16), w_ref[...], preferred_element_type=F32) + bias_ref[...]
        hx = 0.5 * xc
        for d in range(2):
            a, b = _rg_coeffs(zh, hx, sp4, d)
            dec[d][0][c0 * n_rows:c0 * n_rows + tc, :] = a
            dec[d][1][c0 * n_rows:c0 * n_rows + tc, :] = b

    def scan_steps(rl):
        def scan_step(c, carry):
            hf, pf, hb, pb = carry
            tf = c * n_rows + rl
            tb = (GRID_W - 1 - c) * n_rows + (SUBLANES - 1 - rl)
            af = a0[pl.ds(tf, SUBLANES, stride=SUBLANES), :]
            bf = b0[pl.ds(tf, SUBLANES, stride=SUBLANES), :]
            ab = a1[pl.ds(tb, SUBLANES, stride=SUBLANES), :]
            bb = b1[pl.ds(tb, SUBLANES, stride=SUBLANES), :]
            hf = af * hf + bf
            pf = pf * af
            hb = ab * hb + bb
            pb = pb * ab
            b0[pl.ds(tf, SUBLANES, stride=SUBLANES), :] = hf
            a0[pl.ds(tf, SUBLANES, stride=SUBLANES), :] = pf
            b1[pl.ds(tb, SUBLANES, stride=SUBLANES), :] = hb
            a1[pl.ds(tb, SUBLANES, stride=SUBLANES), :] = pb
            return hf, pf, hb, pb
        return scan_step

    z8 = jnp.zeros((SUBLANES, LANES), F32)
    o8 = jnp.ones((SUBLANES, LANES), F32)
    carry = (z8, o8, z8, o8)
    for rl in range(SUBLANES):
        carry = lax.fori_loop(0, GRID_W, scan_steps(rl), carry, unroll=4)
    hf, pf, hb, pb = carry

    cf, cbk = _rg_carries(h0_ref, hl_ref, hf, pf, hb, pb)
    cf4 = jnp.concatenate([jnp.broadcast_to(c, (SUBLANES, LANES)) for c in cf] * 4, axis=0)
    cb4 = jnp.concatenate([jnp.broadcast_to(c, (SUBLANES, LANES)) for c in cbk] * 4, axis=0)
    for k in range(GRID_W // 4):
        j0 = k * tc
        h = (b0[j0:j0 + tc, :] + a0[j0:j0 + tc, :] * cf4 + b1[j0:j0 + tc, :] + a1[j0:j0 + tc, :] * cb4)
        y_ref[j0:j0 + tc, :] = (h * jax.nn.gelu(g_ref[j0:j0 + tc, :])).astype(y_ref.dtype)


def _rg_branch(p, h0, cw, cb, wcat, bcat, lam, *, seq, row_block0, colmajor):
    nblk = W_RG // LANES
    if colmajor:
        kern = _rg_grid_kernel
        scratch = [pltpu.VMEM((seq, LANES), F32)] * 4
    else:
        tc = min(256, seq)
        seg = seq // SUBLANES
        pitch = seg + SUBLANES if (seg // SUBLANES) % 2 == 0 else seg
        kern = functools.partial(_rg_kernel, seq=seq, tc=tc, seg=seg, pitch=pitch)
        scratch = ([pltpu.VMEM((seq + 2 * CONV_PAD, LANES), F32)]
                   + [pltpu.VMEM((SUBLANES * pitch, LANES), F32)] * 4)
    in_specs = [
        pl.BlockSpec((seq, LANES), lambda b, c: (row_block0 + b, COL_XR * nblk + c)),
        pl.BlockSpec((seq, LANES), lambda b, c: (row_block0 + b, COL_RGG * nblk + c)),
        pl.BlockSpec((CONV_W, LANES), lambda b, c: (0, c)),
        pl.BlockSpec((1, LANES), lambda b, c: (0, c)),
        pl.BlockSpec((None, LANES, 4 * LANES), lambda b, c: (c, 0, 0)),
        pl.BlockSpec((None, 1, 4 * LANES), lambda b, c: (c, 0, 0)),
        pl.BlockSpec((2, LANES), lambda b, c: (0, c)),
        pl.BlockSpec((None, 2, LANES), lambda b, c: (b, 0, c)),
    ]
    out_shape = [jax.ShapeDtypeStruct((BATCH * seq, W_RG), BF16), jax.ShapeDtypeStruct((BATCH, 2, W_RG), F32)]
    out_specs = [pl.BlockSpec((seq, LANES), lambda b, c: (b, c)),
                 pl.BlockSpec((None, 2, LANES), lambda b, c: (b, 0, c))]
    return pl.pallas_call(
        kern,
        grid=(BATCH, nblk),
        in_specs=in_specs,
        out_specs=out_specs,
        out_shape=out_shape,
        scratch_shapes=scratch,
        compiler_params=_cparams("arbitrary", "arbitrary"),
        name=f"rglru_{seq}",
    )(p, p, cw, cb, wcat, bcat, lam, h0)


def _qkconv_kernel(lat_ref, ctx_ref, cw_ref, cb_ref, sc_ref, o_ref, xs, *, tc):
    cw = cw_ref[...]
    zeros = jnp.zeros((CONV_PAD, xs.shape[1]), F32)
    xs[0:CONV_PAD, :] = zeros
    for src_ref, out_off, n in ((ctx_ref, 0, CTX_LEN), (lat_ref, CTX_LEN, SEQ)):
        xs[CONV_PAD:CONV_PAD + n, :] = src_ref[...]
        xs[CONV_PAD + n:CONV_PAD + n + CONV_PAD, :] = zeros
        for c in range(n // tc):
            r0 = CONV_PAD + c * tc - 2
            xc = cb_ref[...] + cw[0:1, :] * xs[r0:r0 + tc, :]
            for j in range(1, CONV_W):
                xc = xc + cw[j:j + 1, :] * xs[r0 + j:r0 + j + tc, :]
            y = xc * _sigmoid(xc) * sc_ref[...]
            o_ref[out_off + c * tc:out_off + (c + 1) * tc, :] = y.astype(o_ref.dtype)


def _qkconv(p, cw, cb, scale):
    cwid = 256
    n = 2 * W_M
    col0 = COL_Q * D_MODEL // cwid
    return pl.pallas_call(
        functools.partial(_qkconv_kernel, tc=256),
        grid=(BATCH, n // cwid),
        in_specs=[pl.BlockSpec((SEQ, cwid), lambda b, c: (b, col0 + c)),
                  pl.BlockSpec((CTX_LEN, cwid), lambda b, c: (N_LAT // CTX_LEN + b, col0 + c)),
                  pl.BlockSpec((CONV_W, cwid), lambda b, c: (0, c)),
                  pl.BlockSpec((1, cwid), lambda b, c: (0, c)),
                  pl.BlockSpec((1, cwid), lambda b, c: (0, c))],
        out_specs=pl.BlockSpec((None, SEQ_ALL, cwid), lambda b, c: (b, 0, c)),
        out_shape=jax.ShapeDtypeStruct((BATCH, SEQ_ALL, n), BF16),
        scratch_shapes=[pltpu.VMEM((SEQ + 2 * CONV_PAD, cwid), F32)],
        compiler_params=_cparams("arbitrary", "arbitrary"),
        name="qk_conv",
    )(p, p, cw, cb, scale)


def _lane_cumsum(x, reverse):
    lane = lax.broadcasted_iota(jnp.int32, x.shape, 1)
    s = 1
    while s < LANES:
        if reverse:
            x = x + jnp.where(lane < LANES - s, pltpu.roll(x, LANES - s, axis=1), 0.0)
        else:
            x = x + jnp.where(lane >= s, pltpu.roll(x, s, axis=1), 0.0)
        s *= 2
    return x


def _mlstm_direction(q_ref, k_ref, v_ref, g, o_ref, c_ref, n_ref, m_ref, d):
    reverse = d == 1
    nt = (((1,), (1,)), ((), ()))
    gt = g.T
    base = d * 2 * M_HEADS
    i_rows = gt[base:base + M_HEADS, :]
    f_rows = -_softplus(-gt[base + M_HEADS:base + 2 * M_HEADS, :])
    bcum = _lane_cumsum(f_rows, reverse)
    r_rows = i_rows - bcum
    pad = jnp.zeros((M_CHUNK - M_HEADS, M_CHUNK), F32)
    r_cols = jnp.concatenate([r_rows, pad], axis=0).T
    last = 0 if reverse else M_CHUNK - 1

    src_i = lax.broadcasted_iota(jnp.int32, (M_CHUNK, M_CHUNK), 0)
    dst_i = lax.broadcasted_iota(jnp.int32, (M_CHUNK, M_CHUNK), 1)
    mask = (src_i >= dst_i) if reverse else (src_i <= dst_i)

    for h in range(M_HEADS):
        sl = slice(h * M_HEAD_DIM, (h + 1) * M_HEAD_DIM)
        qh = q_ref[:, sl]
        kh = k_ref[:, sl]
        vt = v_ref[:, sl].T
        b_row = bcum[h:h + 1, :]
        m_prev = m_ref[d, h:h + 1, :]
        b_last = jnp.broadcast_to(b_row[:, last:last + 1], (1, M_CHUNK))

        log_d = jnp.where(mask, b_row + r_cols[:, h:h + 1], -jnp.inf)
        log_inter = b_row + m_prev
        m_t = jnp.maximum(jnp.max(log_d, axis=0, keepdims=True), log_inter)
        st = lax.dot_general(kh, qh, nt, preferred_element_type=F32)
        st = st * jnp.exp(log_d - m_t)
        w_inter = jnp.exp(log_inter - m_t)
        c_prev = c_ref[d, h]
        n_prev = n_ref[d, h]
        cq = lax.dot_general(c_prev.astype(BF16), qh, nt, preferred_element_type=F32)
        num = jnp.dot(vt.astype(BF16), st.astype(BF16), preferred_element_type=F32) + w_inter * cq
        qn = lax.dot_general(n_prev.astype(BF16), qh, nt, preferred_element_type=F32)[0:1, :]
        den = jnp.sum(st, axis=0, keepdims=True) + w_inter * qn
        o_ref[:, sl] = (num / jnp.maximum(jnp.abs(den), jnp.exp(-m_t))).T

        log_w = b_last - b_row + i_rows[h:h + 1, :]
        m_new = jnp.maximum(b_last + m_prev, jnp.max(log_w, axis=1, keepdims=True))
        w = jnp.exp(log_w - m_new)
        decay = jnp.exp(b_last + m_prev - m_new)
        vw = (vt * w).astype(BF16)
        c_ref[d, h] = decay[:, 0:1] * c_prev + jnp.dot(vw, kh, preferred_element_type=F32)
        w8 = jnp.broadcast_to(w, (SUBLANES, M_CHUNK)).astype(BF16)
        n_ref[d, h] = decay[:, 0:1] * n_prev + jnp.dot(w8, kh, preferred_element_type=F32)
        m_ref[d, h:h + 1, :] = m_new


def _mlstm_kernel(qf_ref, kf_ref, vf_ref, gf_ref, qb_ref, kb_ref, vb_ref, gb_ref, bias_ref, of_ref, ob_ref,
                  c_ref, n_ref, m_ref):
    @pl.when(pl.program_id(1) == 0)
    def _():
        c_ref[...] = jnp.zeros(c_ref.shape, F32)
        n_ref[...] = jnp.zeros(n_ref.shape, F32)
        m_ref[...] = jnp.zeros(m_ref.shape, F32)

    bias = bias_ref[...]
    _mlstm_direction(qf_ref, kf_ref, vf_ref, gf_ref[...] + bias, of_ref, c_ref, n_ref, m_ref, 0)
    _mlstm_direction(qb_ref, kb_ref, vb_ref, gb_ref[...] + bias, ob_ref, c_ref, n_ref, m_ref, 1)


def _chunk_order(i, reverse):
    if not reverse:
        return i
    nctx = CTX_LEN // M_CHUNK
    ntot = SEQ_ALL // M_CHUNK
    return jnp.where(i < nctx, nctx - 1 - i, ntot + nctx - 1 - i)


def _token_chunk(b, i, reverse):
    nctx = CTX_LEN // M_CHUNK
    ci = _chunk_order(i, reverse)
    return jnp.where(ci < nctx, N_LAT // M_CHUNK + b * nctx + ci, b * (SEQ // M_CHUNK) + ci - nctx)


def _mlstm(q_and_k, p, g, gate_b):
    def blk(col, reverse):
        return pl.BlockSpec((None, M_CHUNK, W_M), lambda b, i: (b, _chunk_order(i, reverse), col))

    def tok(col, reverse, width=W_M):
        return pl.BlockSpec((M_CHUNK, width), lambda b, i: (_token_chunk(b, i, reverse), col))

    in_specs = [blk(0, False), blk(1, False), tok(COL_V, False), tok(0, False, LANES),
                blk(0, True), blk(1, True), tok(COL_V, True), tok(0, True, LANES),
                pl.BlockSpec((1, LANES), lambda b, i: (0, 0))]
    out = jax.ShapeDtypeStruct((N_TOK, W_M), F32)
    return pl.pallas_call(
        _mlstm_kernel,
        grid=(BATCH, SEQ_ALL // M_CHUNK),
        in_specs=in_specs,
        out_specs=[tok(0, False), tok(0, True)],
        out_shape=[out, out],
        scratch_shapes=[pltpu.VMEM((2, M_HEADS, M_HEAD_DIM, M_HEAD_DIM), F32),
                        pltpu.VMEM((2, M_HEADS, SUBLANES, M_HEAD_DIM), F32),
                        pltpu.VMEM((2, M_HEADS, LANES), F32)],
        compiler_params=_cparams("arbitrary", "arbitrary"),
        name="mlstm",
    )(q_and_k, q_and_k, p, g, q_and_k, q_and_k, p, g, gate_b)


def _merge_kernel(yr_ref, hf_ref, hb_ref, o_ref, ga_ref, gb_ref, gn_ref, prg_ref, pm_ref, mix_ref):
    hs = hf_ref[...] + hb_ref[...]
    hn = jnp.concatenate([_layer_norm(hs[:, h * M_HEAD_DIM:(h + 1) * M_HEAD_DIM]) for h in range(M_HEADS)], axis=1)
    ym = (hn * gn_ref[...] * _sigmoid(o_ref[...])).astype(BF16)
    a = jnp.dot(yr_ref[...], prg_ref[...], preferred_element_type=F32)
    b = jnp.dot(ym, pm_ref[...], preferred_element_type=F32)
    mix_ref[...] = (_sigmoid(ga_ref[...]) * a + _sigmoid(gb_ref[...]) * b).astype(BF16)


def _merge(yr, hf, hb, p1, p2, gn, p_rg, p_m, rows):
    tm = 256
    once = pl.Buffered(1)
    return pl.pallas_call(
        _merge_kernel,
        grid=(rows // tm,),
        in_specs=[pl.BlockSpec((tm, W_RG), lambda i: (i, 0)),
                  pl.BlockSpec((tm, W_M), lambda i: (i, 0)),
                  pl.BlockSpec((tm, W_M), lambda i: (i, 0)),
                  pl.BlockSpec((tm, W_M), lambda i: (i, COL_O)),
                  pl.BlockSpec((tm, D_MODEL), lambda i: (i, 0)),
                  pl.BlockSpec((tm, D_MODEL), lambda i: (i, 1)),
                  pl.BlockSpec((1, W_M), lambda i: (0, 0)),
                  pl.BlockSpec((W_RG, D_MODEL), lambda i: (0, 0), pipeline_mode=once),
                  pl.BlockSpec((W_M, D_MODEL), lambda i: (0, 0), pipeline_mode=once)],
        out_specs=pl.BlockSpec((tm, D_MODEL), lambda i: (i, 0)),
        out_shape=jax.ShapeDtypeStruct((rows, D_MODEL), BF16),
        compiler_params=_cparams("arbitrary"),
        name="merge",
    )(yr, hf, hb, p1, p2, p2, gn, p_rg, p_m)


def _res_ln(h, f, gate, lng, lnb):
    y = _layer_norm(ALPHA * h + gate * f)
    return y * lng + lnb


def _outproj_kernel(a_ref, w_ref, h_ref, g_ref, lng_ref, lnb_ref, o_ref):
    f = jnp.dot(a_ref[...], w_ref[...], preferred_element_type=F32)
    o_ref[...] = _res_ln(h_ref[...], f, g_ref[0], lng_ref[...], lnb_ref[...])


def _outproj(a, w, h, mod, gate_chunk, lng, lnb, rows):
    tm = 256
    return pl.pallas_call(
        _outproj_kernel,
        grid=(rows // tm,),
        in_specs=[pl.BlockSpec((tm, D_MODEL), lambda i: (i, 0)),
                  pl.BlockSpec((D_MODEL, D_MODEL), lambda i: (0, 0)),
                  pl.BlockSpec((tm, D_MODEL), lambda i: (i, 0)),
                  _mod_spec(gate_chunk, tm),
                  pl.BlockSpec((1, D_MODEL), lambda i: (0, 0)),
                  pl.BlockSpec((1, D_MODEL), lambda i: (0, 0))],
        out_specs=pl.BlockSpec((tm, D_MODEL), lambda i: (i, 0)),
        out_shape=jax.ShapeDtypeStruct((rows, D_MODEL), F32),
        compiler_params=_cparams("arbitrary"),
        name="out_proj",
    )(a, w, h, mod, lng, lnb)


def _ffn_kernel(h_ref, sh_ref, sc_ref, g_ref, w1_ref, w3_ref, w2_ref, lng_ref, lnb_ref, o_ref, u_ref, acc_ref):
    f = pl.program_id(1)

    @pl.when(f == 0)
    def _():
        u = _layer_norm(h_ref[...]) * (1.0 + sc_ref[0]) + sh_ref[0]
        u_ref[...] = u.astype(BF16)
        acc_ref[...] = jnp.zeros(acc_ref.shape, F32)

    u = u_ref[...]
    h1 = jnp.dot(u, w1_ref[...], preferred_element_type=F32)
    h3 = jnp.dot(u, w3_ref[...], preferred_element_type=F32)
    act = (h1 * _sigmoid(h1) * h3).astype(BF16)
    acc_ref[...] += jnp.dot(act, w2_ref[...], preferred_element_type=F32)

    @pl.when(f == pl.num_programs(1) - 1)
    def _():
        o_ref[...] = _res_ln(h_ref[...], acc_ref[...], g_ref[0], lng_ref[...], lnb_ref[...])


def _ffn(h, mod, w1, w3, w2, lng, lnb):
    rows = h.shape[0]
    tm, tf = 512, 512
    return pl.pallas_call(
        _ffn_kernel,
        grid=(rows // tm, D_FF // tf),
        in_specs=[pl.BlockSpec((tm, D_MODEL), lambda i, f: (i, 0)),
                  _mod_spec(3, tm), _mod_spec(4, tm), _mod_spec(5, tm),
                  pl.BlockSpec((D_MODEL, tf), lambda i, f: (0, f)),
                  pl.BlockSpec((D_MODEL, tf), lambda i, f: (0, f)),
                  pl.BlockSpec((tf, D_MODEL), lambda i, f: (f, 0)),
                  pl.BlockSpec((1, D_MODEL), lambda i, f: (0, 0)),
                  pl.BlockSpec((1, D_MODEL), lambda i, f: (0, 0))],
        out_specs=pl.BlockSpec((tm, D_MODEL), lambda i, f: (i, 0)),
        out_shape=jax.ShapeDtypeStruct((rows, D_MODEL), F32),
        scratch_shapes=[pltpu.VMEM((tm, D_MODEL), BF16), pltpu.VMEM((tm, D_MODEL), F32)],
        compiler_params=_cparams("arbitrary", "arbitrary"),
        name="ffn_dense",
    )(h, mod, mod, mod, w1, w3, w2, lng, lnb)


def _route_kernel(h_ref, sh_ref, sc_ref, rw_ref, rb_ref, t_ref, idx_ref, gate_ref):
    u = _layer_norm(h_ref[...]) * (1.0 + sc_ref[0]) + sh_ref[0]
    u = u.astype(BF16)
    t_ref[...] = u.astype(F32)
    lane = lax.broadcasted_iota(jnp.int32, (h_ref.shape[0], LANES), 1)
    logits = jnp.dot(u, rw_ref[...], preferred_element_type=F32) + rb_ref[...]
    logits = jnp.where(lane < N_EXPERTS, logits, -jnp.inf)
    v1 = jnp.max(logits, axis=-1, keepdims=True)
    i1 = jnp.min(jnp.where(logits == v1, lane, LANES), axis=-1, keepdims=True)
    rest = jnp.where(lane == i1, -jnp.inf, logits)
    v2 = jnp.max(rest, axis=-1, keepdims=True)
    i2 = jnp.min(jnp.where(rest == v2, lane, LANES), axis=-1, keepdims=True)
    e2 = jnp.exp(v2 - v1)
    g1 = 1.0 / (1.0 + e2)
    idx_ref[...] = jnp.where(lane == 0, i1, jnp.where(lane == 1, i2, 0))
    gate_ref[...] = jnp.where(lane == 0, g1, jnp.where(lane == 1, e2 * g1, 0.0))


def _route(h, mod, rw, rb):
    rows = h.shape[0]
    tm = 512
    return pl.pallas_call(
        _route_kernel,
        grid=(rows // tm,),
        in_specs=[pl.BlockSpec((tm, D_MODEL), lambda i: (i, 0)),
                  _mod_spec(3, tm), _mod_spec(4, tm),
                  pl.BlockSpec((D_MODEL, LANES), lambda i: (0, 0)),
                  pl.BlockSpec((1, LANES), lambda i: (0, 0))],
        out_specs=[pl.BlockSpec((tm, D_MODEL), lambda i: (i, 0)),
                   pl.BlockSpec((tm, LANES), lambda i: (i, 0)),
                   pl.BlockSpec((tm, LANES), lambda i: (i, 0))],
        out_shape=[jax.ShapeDtypeStruct((rows, D_MODEL), F32), jax.ShapeDtypeStruct((rows, LANES), jnp.int32),
                   jax.ShapeDtypeStruct((rows, LANES), F32)],
        compiler_params=_cparams("arbitrary"),
        name="moe_route",
    )(h, mod, mod, rw, rb)


def _gather_start(src_hbm, idx_ref, idx0, dst_ref, n, sem):
    def issue(r2, carry):
        for prio in range(2):
            r = 2 * r2 + prio
            row = idx_ref[0, 0, idx0 + r]
            pltpu.make_async_copy(src_hbm.at[pl.ds(row, 1)], dst_ref.at[pl.ds(r, 1)], sem).start(priority=prio)
        return carry

    lax.fori_loop(0, n // 2, issue, 0)


def _gather_wait(src_hbm, dst_ref, n, sem):
    pltpu.make_async_copy(src_hbm.at[pl.ds(0, n)], dst_ref, sem).wait()


def _experts_kernel(te_ref, tv_ref, tok_ref, nxt_ref, t_hbm, w1_ref, w3_ref, w2_ref, o_ref,
                    xf_ref, x_ref, acc_ref, sem):
    i = pl.program_id(0)
    f = pl.program_id(1)
    last_tile = pl.num_programs(0) - 1
    live = tv_ref[i] > 0
    next_live = jnp.logical_and(i < last_tile, tv_ref[jnp.minimum(i + 1, last_tile)] > 0)
    slot = i % 2

    @pl.when(jnp.logical_and(live, jnp.logical_and(f == 0, i == 0)))
    def _():
        _gather_start(t_hbm, tok_ref, 0, xf_ref.at[0], MOE_TILE, sem.at[0])

    @pl.when(jnp.logical_and(live, f == 0))
    def _():
        _gather_wait(t_hbm, xf_ref.at[slot], MOE_TILE, sem.at[slot])
        x_ref[...] = xf_ref[slot].astype(BF16)
        acc_ref[...] = jnp.zeros(acc_ref.shape, F32)

    @pl.when(jnp.logical_and(next_live, f == 0))
    def _():
        _gather_start(t_hbm, nxt_ref, 0, xf_ref.at[1 - slot], MOE_TILE, sem.at[1 - slot])

    @pl.when(live)
    def _():
        x = x_ref[...]
        h1 = jnp.dot(x, w1_ref[...], preferred_element_type=F32)
        h3 = jnp.dot(x, w3_ref[...], preferred_element_type=F32)
        act = (h1 * _sigmoid(h1) * h3).astype(BF16)
        acc_ref[...] += jnp.dot(act, w2_ref[...], preferred_element_type=F32)

    @pl.when(f == pl.num_programs(1) - 1)
    def _():
        o_ref[...] = jnp.where(live, acc_ref[...], 0.0)


def _experts(tile_expert, tile_live, row_token, t, w1, w3, w2):
    tf = 512
    grid_spec = pltpu.PrefetchScalarGridSpec(
        num_scalar_prefetch=2,
        grid=(MOE_TILES, D_FF // tf),
        in_specs=[pl.BlockSpec((1, 1, MOE_TILE), lambda i, f, te, tv: (i, 0, 0), memory_space=pltpu.SMEM),
                  pl.BlockSpec((1, 1, MOE_TILE), lambda i, f, te, tv: (jnp.minimum(i + 1, MOE_TILES - 1), 0, 0),
                               memory_space=pltpu.SMEM),
                  pl.BlockSpec(memory_space=pl.ANY),
                  pl.BlockSpec((None, D_MODEL, tf), lambda i, f, te, tv: (te[i], 0, f)),
                  pl.BlockSpec((None, D_MODEL, tf), lambda i, f, te, tv: (te[i], 0, f)),
                  pl.BlockSpec((None, tf, D_MODEL), lambda i, f, te, tv: (te[i], f, 0))],
        out_specs=pl.BlockSpec((MOE_TILE, D_MODEL), lambda i, f, te, tv: (i, 0)),
        scratch_shapes=[pltpu.VMEM((2, MOE_TILE, D_MODEL), F32), pltpu.VMEM((MOE_TILE, D_MODEL), BF16),
                        pltpu.VMEM((MOE_TILE, D_MODEL), F32), pltpu.SemaphoreType.DMA((2,))],
    )
    return pl.pallas_call(
        _experts_kernel,
        grid_spec=grid_spec,
        out_shape=jax.ShapeDtypeStruct((MOE_TILES * MOE_TILE, D_MODEL), F32),
        compiler_params=_cparams("arbitrary", "arbitrary"),
        name="moe_experts",
    )(tile_expert, tile_live, row_token, row_token, t, w1, w3, w2)


def _combine_kernel(pos_ref, nxt_ref, y_hbm, g_ref, h_ref, gm_ref, lng_ref, lnb_ref, o_ref, buf_ref, sem):
    i = pl.program_id(0)
    tm = h_ref.shape[0]
    slot = i % 2

    @pl.when(i == 0)
    def _():
        _gather_start(y_hbm, pos_ref, 0, buf_ref.at[0], 2 * tm, sem.at[0])

    @pl.when(i < pl.num_programs(0) - 1)
    def _():
        _gather_start(y_hbm, nxt_ref, 0, buf_ref.at[1 - slot], 2 * tm, sem.at[1 - slot])

    _gather_wait(y_hbm, buf_ref.at[slot], 2 * tm, sem.at[slot])
    g = g_ref[...]
    f = g[:, 0:1] * buf_ref[slot, 0:tm, :] + g[:, 1:2] * buf_ref[slot, tm:2 * tm, :]
    o_ref[...] = _res_ln(h_ref[...], f, gm_ref[0], lng_ref[...], lnb_ref[...])


def _combine(pos, yb, gates, h, mod, lng, lnb):
    rows = h.shape[0]
    tm = pos.shape[-1] // 2
    nt = rows // tm
    return pl.pallas_call(
        _combine_kernel,
        grid=(nt,),
        in_specs=[pl.BlockSpec((1, 1, 2 * tm), lambda i: (i, 0, 0), memory_space=pltpu.SMEM),
                  pl.BlockSpec((1, 1, 2 * tm), lambda i: (jnp.minimum(i + 1, nt - 1), 0, 0), memory_space=pltpu.SMEM),
                  pl.BlockSpec(memory_space=pl.ANY),
                  pl.BlockSpec((tm, LANES), lambda i: (i, 0)),
                  pl.BlockSpec((tm, D_MODEL), lambda i: (i, 0)),
                  _mod_spec(5, tm),
                  pl.BlockSpec((1, D_MODEL), lambda i: (0, 0)),
                  pl.BlockSpec((1, D_MODEL), lambda i: (0, 0))],
        out_specs=pl.BlockSpec((tm, D_MODEL), lambda i: (i, 0)),
        out_shape=jax.ShapeDtypeStruct((rows, D_MODEL), F32),
        scratch_shapes=[pltpu.VMEM((2, 2 * tm, D_MODEL), F32), pltpu.SemaphoreType.DMA((2,))],
        compiler_params=_cparams("arbitrary"),
        name="moe_combine",
    )(pos, pos, yb, gates, h, mod, lng, lnb)


def _cast_kernel(w_ref, o_ref):
    o_ref[...] = w_ref[...].astype(BF16)


def _to_bf16(w, tr):
    r, c = w.shape
    return pl.pallas_call(
        _cast_kernel,
        grid=(r // tr,),
        in_specs=[pl.BlockSpec((tr, c), lambda i: (i, 0))],
        out_specs=pl.BlockSpec((tr, c), lambda i: (i, 0)),
        out_shape=jax.ShapeDtypeStruct((r, c), BF16),
        compiler_params=_cparams("arbitrary"),
        name="cast_bf16",
    )(w)


def _moe(h, mod, router_w, router_b, w1, w3, w2, lng, lnb):
    n = h.shape[0]
    rw = jnp.zeros((D_MODEL, LANES), BF16).at[:, :N_EXPERTS].set(router_w.astype(BF16))
    rb = jnp.zeros((1, LANES), F32).at[0, :N_EXPERTS].set(router_b)
    t, idx, gates = _route(h, mod, rw, rb)
    expert = idx[:, :TOP_K].reshape(-1)
    onehot = (expert[:, None] == jnp.arange(N_EXPERTS)[None, :]).astype(jnp.int32)
    csum = jnp.cumsum(onehot, axis=0)
    rank = jnp.sum(csum * onehot, axis=1) - 1
    counts = csum[-1]
    padded = (counts + MOE_TILE - 1) // MOE_TILE * MOE_TILE
    pend = jnp.cumsum(padded)
    pstart = pend - padded
    pos = pstart[expert] + rank
    tile_start = jnp.arange(MOE_TILES) * MOE_TILE
    tile_expert = jnp.minimum(jnp.searchsorted(pend, tile_start, side='right'), N_EXPERTS - 1).astype(jnp.int32)
    tile_live = (tile_start < pend[-1]).astype(jnp.int32)
    order = jnp.argsort(expert, stable=True).astype(jnp.int32)
    row_expert = jnp.repeat(tile_expert, MOE_TILE)
    k = jnp.arange(MOE_TILES * MOE_TILE, dtype=jnp.int32) - pstart[row_expert].astype(jnp.int32)
    sorted_at = jnp.clip((jnp.cumsum(counts) - counts)[row_expert].astype(jnp.int32) + k, 0, n * TOP_K - 1)
    row_token = jnp.where(k < counts[row_expert], order[sorted_at] // TOP_K, 0).astype(jnp.int32)
    yb = _experts(tile_expert, tile_live, row_token.reshape(MOE_TILES, 1, MOE_TILE), t, w1, w3, w2)
    tm = 256
    pos2 = pos.reshape(n // tm, tm, TOP_K).transpose(0, 2, 1).reshape(n // tm, 1, TOP_K * tm).astype(jnp.int32)
    return _combine(pos2, yb, gates, h, mod, lng, lnb)


def _swap_grid_order(t):
    b, l, c = t.shape
    return t.reshape(b, l // GRID_W, GRID_W, c).transpose(0, 2, 1, 3).reshape(b, l, c)


def _mixer(h, mod, w_in_all, layer, conv_rg_w, conv_rg_b, conv_m_w, conv_m_b, rg_wa, rg_ba, rg_wx, rg_bx, rg_lam,
           m_gate_b, m_gn_g, p_rg, p_m, rows_out):
    gate0 = C_M + 4 * W_M
    w_tail = w_in_all[layer, :, gate0:]
    w_gate = jnp.zeros((D_MODEL, LANES), BF16).at[:, :N_GATES].set(w_tail[:, :N_GATES].astype(BF16))
    u, pg = _lnmod(h, mod, w_gate)
    p = _proj(u, w_in_all[layer:layer + 1, :, :gate0], 0, gate0)
    p2 = _proj(u, w_tail[None, :, N_GATES:], 0, 2 * D_MODEL)

    wcat = (0.5 * jnp.concatenate([rg_wa[0], rg_wx[0], rg_wa[1], rg_wx[1]], axis=-1)).astype(BF16)
    bcat = 0.5 * jnp.stack([rg_ba[0], rg_bx[0], rg_ba[1], rg_bx[1]], axis=0)
    bcat = bcat.reshape(4, RG_BLOCKS, RG_BLOCK).transpose(1, 0, 2).reshape(RG_BLOCKS, 1, 4 * RG_BLOCK)
    cb = conv_rg_b.reshape(1, W_RG)
    h0 = jnp.zeros((BATCH, 2, W_RG), F32)
    yr_c, st = _rg_branch(p, h0, conv_rg_w, cb, wcat, bcat, rg_lam, seq=CTX_LEN, row_block0=N_LAT // CTX_LEN,
                          colmajor=False)
    yr, _ = _rg_branch(p, st, conv_rg_w, cb, wcat, bcat, rg_lam, seq=SEQ, row_block0=0, colmajor=True)
    if rows_out > N_LAT:
        yr = jnp.concatenate([yr, yr_c], axis=0)

    qscale = jnp.concatenate([jnp.full((1, W_M), M_HEAD_DIM ** -0.5, F32), jnp.ones((1, W_M), F32)], axis=1)
    qk = _qkconv(p, conv_m_w, conv_m_b.reshape(1, 2 * W_M), qscale)
    gate_b = jnp.zeros((1, LANES), F32).at[0, :N_GATES].set(m_gate_b.reshape(-1))
    h_f, h_b = _mlstm(qk, p, pg, gate_b)

    return _merge(yr, h_f, h_b, p, p2, m_gn_g.reshape(1, W_M), p_rg.astype(BF16), p_m.astype(BF16), rows_out)


def kernel(x, c, ctx, c_ctx, w_mod, b_mod, w_in, conv_rg_w, conv_rg_b, conv_m_w, conv_m_b, rg_wa, rg_ba, rg_wx, rg_bx, rg_lam, m_gate_b, m_gn_g, p_rg, p_m, w_out, ln_g, ln_b, ff_w1, ff_w3, ff_w2, router_w, router_b, ex_w1, ex_w3, ex_w2):
    h = jnp.concatenate([_swap_grid_order(x).reshape(N_LAT, D_MODEL), ctx.reshape(N_CTX, D_MODEL)], axis=0)
    cond = jnp.zeros((SUBLANES, D_MODEL), F32).at[:BATCH].set(c).at[BATCH].set(c_ctx)
    for l in range(DEPTH):
        last = l == DEPTH - 1
        rows = N_LAT if last else N_TOK
        mod = _modvec(cond, w_mod, b_mod[l], l)
        mix = _mixer(h, mod, w_in, l, conv_rg_w[l], conv_rg_b[l], conv_m_w[l], conv_m_b[l], rg_wa[l], rg_ba[l],
                     rg_wx[l], rg_bx[l], rg_lam[l], m_gate_b[l], m_gn_g[l], p_rg[l], p_m[l], rows)
        lng = ln_g[l].reshape(2, 1, D_MODEL)
        lnb = ln_b[l].reshape(2, 1, D_MODEL)
        h = _outproj(mix, w_out[l].astype(BF16), h, mod, 2, lng[0], lnb[0], rows)
        k = l // 2
        if l % 2 == 0:
            h = _ffn(h, mod, _to_bf16(ff_w1[k], 256), _to_bf16(ff_w3[k], 256), _to_bf16(ff_w2[k], 1024),
                     lng[1], lnb[1])
        else:
            w1 = _to_bf16(ex_w1[k].reshape(N_EXPERTS * D_MODEL, D_FF), 256).reshape(N_EXPERTS, D_MODEL, D_FF)
            w3 = _to_bf16(ex_w3[k].reshape(N_EXPERTS * D_MODEL, D_FF), 256).reshape(N_EXPERTS, D_MODEL, D_FF)
            w2 = _to_bf16(ex_w2[k].reshape(N_EXPERTS * D_FF, D_MODEL), 1024).reshape(N_EXPERTS, D_FF, D_MODEL)
            h = _moe(h, mod, router_w[k], router_b[k], w1, w3, w2, lng[1], lnb[1])
    return _swap_grid_order(h[:N_LAT].reshape(BATCH, SEQ, D_MODEL))
```
